```python
import math
import jax
import jax.numpy as jnp
from jax import lax
import numpy as np

D_MODEL = 1024
BATCH = 2
SEQ = 16384
DEPTH = 2

N_HEADS = 16
HEAD_DIM = D_MODEL // N_HEADS
D_FF = 4 * D_MODEL
ROPE_THETA = 500000.0
ROPE_DIM = HEAD_DIM // 4
BLOCK = 128
DILATED_BRANCHES = ((128, 1), (512, 4), (2048, 16))
N_MIXERS = 2
DEEPNORM_ALPHA = (2 * DEPTH) ** 0.25
DEEPNORM_BETA = (8 * DEPTH) ** -0.25
LN_EPS = 1e-5

kernel_name = "hybrid_stickbreak_dilated_deepnorm"


def layer_norm(x, g, b):
    xf = x.astype(jnp.float32)
    mu = jnp.mean(xf, axis=-1, keepdims=True)
    var = jnp.mean(jnp.square(xf - mu), axis=-1, keepdims=True)
    y = (xf - mu) * lax.rsqrt(var + LN_EPS)
    return (y * g.astype(jnp.float32) + b.astype(jnp.float32)).astype(x.dtype)


def rotary_partial(x, positions):
    half = ROPE_DIM // 2
    inv_freq = ROPE_THETA ** (-jnp.arange(half, dtype=jnp.float32) / half)
    ang = positions.astype(jnp.float32)[:, None] * inv_freq[None, :]
    cos = jnp.cos(ang)[None, :, None, :]
    sin = jnp.sin(ang)[None, :, None, :]
    xf = x.astype(jnp.float32)
    x1 = xf[..., :half]
    x2 = xf[..., half:ROPE_DIM]
    return jnp.concatenate([x1 * cos - x2 * sin, x2 * cos + x1 * sin, xf[..., ROPE_DIM:]], axis=-1)


def stick_breaking_attention(q, k, v):
    B, H, S, hd = q.shape
    nq = S // BLOCK
    scale = 1.0 / math.sqrt(hd)
    ar = jnp.arange(BLOCK)
    tri_incl = (ar[:, None] >= ar[None, :]).astype(jnp.float32)
    outs = []
    for blk in range(nq):
        n_k = blk + 1
        end = n_k * BLOCK
        q_blk = q[:, :, blk * BLOCK:end]
        k_blk = k[:, :, :end]
        v_blk = v[:, :, :end]
        z = jnp.einsum('bhqd,bhkd->bhqk', q_blk, k_blk) * scale
        causal = jnp.arange(end)[None, :] < (blk * BLOCK + ar)[:, None]
        log_fail = jnp.where(causal, -jax.nn.softplus(z), 0.0)
        lf = log_fail.reshape(B, H, BLOCK, n_k, BLOCK)
        r_in = jnp.einsum('bhqcj,js->bhqcs', lf, tri_incl)
        totals = jnp.sum(lf, axis=-1)
        ac = jnp.arange(n_k)
        later = (ac[:, None] > ac[None, :]).astype(jnp.float32)
        r_cross = jnp.einsum('bhqe,ec->bhqc', totals, later)
        log_w = z.reshape(B, H, BLOCK, n_k, BLOCK) + r_in + r_cross[..., None]
        w = jnp.where(causal, jnp.exp(log_w).reshape(B, H, BLOCK, end), 0.0)
        outs.append(jnp.einsum('bhqk,bhkd->bhqd', w, v_blk))
    out = jnp.concatenate(outs, axis=2)
    return out.transpose(0, 2, 1, 3)


def dilated_branch(q, k, v, window, dilation):
    B, H, S, hd = q.shape
    w_sub = window // dilation
    span = dilation * BLOCK
    s_pad = -(-S // span) * span
    L = s_pad // dilation
    nb = L // BLOCK

    def to_sub(t):
        t = jnp.pad(t, ((0, 0), (0, 0), (0, s_pad - S), (0, 0)))
        t = t.reshape(B, H, L, dilation, hd).transpose(0, 1, 3, 2, 4)
        return t.reshape(B, H, dilation, nb, BLOCK, hd)

    def with_prev(t):
        prev = jnp.pad(t, ((0, 0), (0, 0), (0, 0), (1, 0), (0, 0), (0, 0)))[:, :, :, :-1]
        return jnp.concatenate([prev, t], axis=4)

    qs = to_sub(q)
    kk = with_prev(to_sub(k))
    vv = with_prev(to_sub(v))
    s = jnp.einsum('bhrnqd,bhrnkd->bhrnqk', qs, kk)
    qi = jnp.arange(BLOCK)[:, None]
    kj = jnp.arange(2 * BLOCK)[None, :]
    dist = BLOCK + qi - kj
    blk = jnp.arange(nb)[:, None, None]
    valid = (dist >= 0) & (dist <= w_sub) & ((blk > 0) | (kj >= BLOCK))
    s = jnp.where(valid, s, -jnp.inf)
    m = jnp.max(s, axis=-1, keepdims=True)
    p = jnp.exp(s - m)
    den = jnp.sum(p, axis=-1, keepdims=True)
    num = jnp.einsum('bhrnqk,bhrnkd->bhrnqd', p, vv)

    def from_sub(t):
        c = t.shape[-1]
        t = t.reshape(B, H, dilation, L, c).transpose(0, 1, 3, 2, 4).reshape(B, H, s_pad, c)
        return t[:, :, :S]

    return from_sub(m), from_sub(den), from_sub(num)


def dilated_attention(q, k, v):
    hd = q.shape[-1]
    qs = q * (1.0 / math.sqrt(hd))
    parts = [dilated_branch(qs, k, v, w, d) for (w, d) in DILATED_BRANCHES]
    m_all = jnp.max(jnp.concatenate([p[0] for p in parts], axis=-1), axis=-1, keepdims=True)
    num = sum(jnp.exp(p[0] - m_all) * p[2] for p in parts)
    den = sum(jnp.exp(p[0] - m_all) * p[1] for p in parts)
    out = num / den
    return out.transpose(0, 2, 1, 3)


def setup_inputs(seed: int = 0) -> dict:
    key = jax.random.key(seed)
    keys = jax.random.split(key, 1 + 9 * DEPTH)
    inputs = {"x": jax.random.normal(keys[0], (BATCH, SEQ, D_MODEL), jnp.float32)}
    for i in range(DEPTH):
        k = keys[1 + 9 * i: 10 + 9 * i]
        w_qk = jax.random.normal(k[0], (D_MODEL, 2 * D_MODEL), jnp.float32) * D_MODEL ** -0.5
        w_v = jax.random.normal(k[1], (D_MODEL, D_MODEL), jnp.float32) * (D_MODEL ** -0.5 * DEEPNORM_BETA)
        inputs[f"w_qkv_{i}"] = jnp.concatenate([w_qk, w_v], axis=1)
        inputs[f"w_o_{i}"] = jax.random.normal(k[2], (D_MODEL, D_MODEL), jnp.float32) * (D_MODEL ** -0.5 * DEEPNORM_BETA)
        inputs[f"ln1_g_{i}"] = 1.0 + 0.02 * jax.random.normal(k[3], (D_MODEL,), jnp.float32)
        inputs[f"ln1_b_{i}"] = 0.02 * jax.random.normal(k[4], (D_MODEL,), jnp.float32)
        inputs[f"w_ff1_{i}"] = jax.random.normal(k[5], (D_MODEL, D_FF), jnp.float32) * (D_MODEL ** -0.5 * DEEPNORM_BETA)
        inputs[f"w_ff2_{i}"] = jax.random.normal(k[6], (D_FF, D_MODEL), jnp.float32) * (D_FF ** -0.5 * DEEPNORM_BETA)
        inputs[f"ln2_g_{i}"] = 1.0 + 0.02 * jax.random.normal(k[7], (D_MODEL,), jnp.float32)
        inputs[f"ln2_b_{i}"] = 0.02 * jax.random.normal(k[8], (D_MODEL,), jnp.float32)
    return inputs


def reference(x,
              w_qkv_0, w_o_0, ln1_g_0, ln1_b_0, w_ff1_0, w_ff2_0, ln2_g_0, ln2_b_0,
              w_qkv_1, w_o_1, ln1_g_1, ln1_b_1, w_ff1_1, w_ff2_1, ln2_g_1, ln2_b_1):
    layers = (
        (w_qkv_0, w_o_0, ln1_g_0, ln1_b_0, w_ff1_0, w_ff2_0, ln2_g_0, ln2_b_0),
        (w_qkv_1, w_o_1, ln1_g_1, ln1_b_1, w_ff1_1, w_ff2_1, ln2_g_1, ln2_b_1),
    )
    B, S, D = x.shape
    positions = jnp.arange(S, dtype=jnp.int32)
    for i in range(DEPTH):
        w_qkv, w_o, g1, b1, w1, w2, g2, b2 = layers[i]
        qkv = jnp.einsum('bsd,de->bse', x, w_qkv).reshape(B, S, 3, N_HEADS, HEAD_DIM)
        q, k, v = qkv[:, :, 0], qkv[:, :, 1], qkv[:, :, 2]
        if i % N_MIXERS == 0:
            qh = q.astype(jnp.float32).transpose(0, 2, 1, 3)
            kh = k.astype(jnp.float32).transpose(0, 2, 1, 3)
            vh = v.astype(jnp.float32).transpose(0, 2, 1, 3)
            o = stick_breaking_attention(qh, kh, vh)
        else:
            qh = rotary_partial(q, positions).transpose(0, 2, 1, 3)
            kh = rotary_partial(k, positions).transpose(0, 2, 1, 3)
            vh = v.astype(jnp.float32).transpose(0, 2, 1, 3)
            o = dilated_attention(qh, kh, vh)
        mix = jnp.einsum('bse,ed->bsd', o.reshape(B, S, D).astype(x.dtype), w_o)
        x = layer_norm(DEEPNORM_ALPHA * x + mix, g1, b1)
        h = jnp.square(jax.nn.relu(jnp.einsum('bsd,df->bsf', x, w1)))
        x = layer_norm(DEEPNORM_ALPHA * x + jnp.einsum('bsf,fd->bsd', h, w2), g2, b2)
    return x
```

```python
import functools
import math

import jax
import jax.numpy as jnp
from jax import lax
from jax.experimental import pallas as pl
from jax.experimental.pallas import tpu as pltpu

D_MODEL = 1024
N_HEADS = 16
HEAD_DIM = 64
D_FF = 4096
ROPE_THETA = 500000.0
ROPE_DIM = 16
DEPTH = 2
ALPHA = (2 * DEPTH) ** 0.25
LN_EPS = 1e-5
WINDOW_SUB = 128
N_RES = 16

LANES = 128
N_PAIRS = D_MODEL // LANES
N_COLS = 3 * N_PAIRS
VMEM_LIMIT = 56 * 1024 * 1024

SB_DONE = 120.0

F32 = jnp.float32
BF16 = jnp.bfloat16


def _const_spec(shape):
    nd = len(shape)
    return pl.BlockSpec(shape, lambda *_: (0,) * nd, pipeline_mode=pl.Buffered(1))


def _qkv_kernel(*refs, rotary):
    if rotary:
        x_ref, w_ref, c_ref, sa_ref, sb_ref, o_ref = refs
    else:
        x_ref, w_ref, o_ref = refs
    x = x_ref[0].astype(BF16)
    for part in range(3):
        y = jnp.dot(x, w_ref[:, part * D_MODEL:(part + 1) * D_MODEL],
                    preferred_element_type=F32)
        for j in range(N_PAIRS):
            col = y[:, j * LANES:(j + 1) * LANES]
            if rotary and part < 2:
                col = (col * c_ref[0, 0]
                       + pltpu.roll(col, LANES - ROPE_DIM // 2, 1) * sa_ref[0, 0]
                       + pltpu.roll(col, ROPE_DIM // 2, 1) * sb_ref[0, 0])
            if part == 0:
                col = col * (1.0 / math.sqrt(HEAD_DIM))
            o_ref[0, part * N_PAIRS + j, 0] = col.astype(BF16)


def _qkv_proj(x, w_bf16, n_res, tables=None, tm=512):
    B, S, D = x.shape
    L = S // n_res
    xv = x.reshape(B, L, n_res * D)
    rotary = tables is not None
    in_specs = [
        pl.BlockSpec((1, tm, D), lambda b, r, i: (b, i, r)),
        _const_spec((D, 3 * D)),
    ]
    args = [xv, w_bf16]
    if rotary:
        for t in tables:
            in_specs.append(pl.BlockSpec((1, 1, tm, LANES), lambda b, r, i: (0, r, i, 0)))
            args.append(t.reshape(1, n_res, L, LANES))
    return pl.pallas_call(
        functools.partial(_qkv_kernel, rotary=rotary),
        out_shape=jax.ShapeDtypeStruct((B, N_COLS, n_res, L, LANES), BF16),
        grid=(B, n_res, L // tm),
        in_specs=in_specs,
        out_specs=pl.BlockSpec((1, N_COLS, 1, tm, LANES), lambda b, r, i: (b, 0, r, i, 0)),
        compiler_params=pltpu.CompilerParams(
            dimension_semantics=("parallel", "parallel", "parallel"),
            vmem_limit_bytes=VMEM_LIMIT),
        name="qkv_rot" if rotary else "qkv",
    )(*args)


def _softplus(z):
    return jnp.maximum(z, 0.0) + jnp.log(1.0 + jnp.exp(-jnp.abs(z)))


def _sb_kernel(q_ref, k_ref, v_ref, o_ref, acc_ref, c_ref, *, tq, tk):
    S = q_ref.shape[3]
    nq = S // tq
    ratio = tq // tk
    lane = lax.broadcasted_iota(jnp.int32, (1, LANES), 1)
    row = lax.broadcasted_iota(jnp.int32, (tq, tk), 0)
    col = lax.broadcasted_iota(jnp.int32, (tq, tk), 1)
    kr = lax.broadcasted_iota(jnp.int32, (tk, tk), 0)
    kc = lax.broadcasted_iota(jnp.int32, (tk, tk), 1)
    tri = (kr >= kc).astype(BF16)

    def q_block(i, _):
        q0 = pl.multiple_of(i * tq, tq)
        q = q_ref[0, 0, 0, pl.ds(q0, tq), :]
        acc_ref[...] = jnp.zeros_like(acc_ref)
        for h in range(2):
            hmask = (lane >= HEAD_DIM) if h else (lane < HEAD_DIM)
            qh = jnp.where(hmask, q, jnp.zeros_like(q))
            c_ref[...] = jnp.zeros_like(c_ref)

            def cond(carry):
                j, cmin = carry
                return jnp.logical_and(j >= 0, cmin < SB_DONE)

            def body(carry):
                j, _ = carry
                k0 = pl.multiple_of(j * tk, tk)
                kc_ = k_ref[0, 0, 0, pl.ds(k0, tk), :]
                vc_ = v_ref[0, 0, 0, pl.ds(k0, tk), :]
                z = lax.dot_general(qh, kc_, (((1,), (1,)), ((), ())),
                                    preferred_element_type=F32)
                causal = (k0 + col) < (q0 + row)
                sp = jnp.where(causal, _softplus(z), 0.0)
                hi = sp.astype(BF16)
                lo = (sp - hi.astype(F32)).astype(BF16)
                r = (jnp.dot(hi, tri, preferred_element_type=F32)
                     + jnp.dot(lo, tri, preferred_element_type=F32))
                c = c_ref[...]
                logw = z - r - c
                w = jnp.where(causal, jnp.exp(logw), 0.0)
                vh = jnp.where(hmask, vc_, jnp.zeros_like(vc_))
                acc_ref[...] += jnp.dot(w.astype(BF16), vh, preferred_element_type=F32)
                c_new = c + jnp.sum(sp, axis=1, keepdims=True)
                c_ref[...] = c_new
                return j - 1, jnp.min(c_new)

            lax.while_loop(cond, body, (i * ratio + ratio - 1, jnp.float32(0.0)))
        o_ref[0, 0, 0, pl.ds(q0, tq), :] = acc_ref[...].astype(BF16)
        return 0

    lax.fori_loop(0, nq, q_block, 0)


def _stick_breaking(qkv, tq=256, tk=256):
    B, _, _, S, _ = qkv.shape
    blk = (1, 1, 1, S, LANES)
    return pl.pallas_call(
        functools.partial(_sb_kernel, tq=tq, tk=tk),
        out_shape=jax.ShapeDtypeStruct((B, N_PAIRS, 1, S, LANES), BF16),
        grid=(B, N_PAIRS),
        in_specs=[
            pl.BlockSpec(blk, lambda b, p: (b, p, 0, 0, 0)),
            pl.BlockSpec(blk, lambda b, p: (b, N_PAIRS + p, 0, 0, 0)),
            pl.BlockSpec(blk, lambda b, p: (b, 2 * N_PAIRS + p, 0, 0, 0)),
        ],
        out_specs=pl.BlockSpec(blk, lambda b, p: (b, p, 0, 0, 0)),
        scratch_shapes=[pltpu.VMEM((tq, LANES), F32), pltpu.VMEM((tq, 1), F32)],
        compiler_params=pltpu.CompilerParams(
            dimension_semantics=("parallel", "parallel"),
            vmem_limit_bytes=VMEM_LIMIT),
        name="stick_breaking",
    )(qkv, qkv, qkv)


_DIL_TILES = ((1, 128), (4, 32), (16, 16))


def _dil_bias(n_slab, q_rows, first):
    tq, tk = n_slab * q_rows, 2 * n_slab * q_rows
    qi = lax.broadcasted_iota(jnp.int32, (tq, tk), 0)
    kj = lax.broadcasted_iota(jnp.int32, (tq, tk), 1)
    sh = q_rows.bit_length() - 1
    a, i = qi >> sh, qi & (q_rows - 1)
    a2, j = kj >> (sh + 1), kj & (2 * q_rows - 1)
    dq = 0 if first else q_rows
    dist = n_slab * (dq + i - j) + (a - a2)
    ok = jnp.logical_and(dist >= 0, dist <= WINDOW_SUB)
    return jnp.where(ok, 0.0, -jnp.inf).astype(F32)


def _dil_kernel(q_ref, k_ref, v_ref, o_ref, b16_ref, b4_ref, b1_ref, m_s, l_s, a_s):
    L = q_ref.shape[3]
    lane = lax.broadcasted_iota(jnp.int32, (1, LANES), 1)
    head0 = lane < HEAD_DIM
    bias_refs = (b16_ref, b4_ref, b1_ref)
    for (n_slab, q_rows), b_ref in zip(_DIL_TILES, bias_refs):
        b_ref[0] = _dil_bias(n_slab, q_rows, False)
        b_ref[1] = _dil_bias(n_slab, q_rows, True)

    def tile(q, k, v, bias):
        ms, ls, num = [], [], None
        for h in range(2):
            hmask = head0 if h == 0 else jnp.logical_not(head0)
            qh = jnp.where(hmask, q, jnp.zeros_like(q))
            s = lax.dot_general(qh, k, (((1,), (1,)), ((), ())),
                                preferred_element_type=F32) + bias
            m = jnp.max(s, axis=1, keepdims=True)
            p = jnp.exp(s - m)
            ls.append(jnp.sum(p, axis=1, keepdims=True))
            ms.append(m)
            vh = jnp.where(hmask, v, jnp.zeros_like(v))
            pv = jnp.dot(p.astype(BF16), vh, preferred_element_type=F32)
            num = pv if num is None else num + pv
        return jnp.where(head0, ms[0], ms[1]), jnp.where(head0, ls[0], ls[1]), num

    def branch_tile(branch, slabs, lq0, out_off):
        n_slab, q_rows = _DIL_TILES[branch]
        lq0 = pl.multiple_of(lq0, q_rows)
        first = lq0 < q_rows
        kq0 = pl.multiple_of(jnp.maximum(lq0 - q_rows, 0), q_rows)
        q = jnp.concatenate([q_ref[0, 0, s, pl.ds(lq0, q_rows), :] for s in slabs], axis=0)
        k = jnp.concatenate([k_ref[0, 0, s, pl.ds(kq0, 2 * q_rows), :] for s in slabs], axis=0)
        v = jnp.concatenate([v_ref[0, 0, s, pl.ds(kq0, 2 * q_rows), :] for s in slabs], axis=0)
        bias = bias_refs[branch][first.astype(jnp.int32)]
        m, l, num = tile(q, k, v, bias)
        out_off = pl.multiple_of(out_off, q_rows)
        for n, s in enumerate(slabs):
            rows = slice(n * q_rows, (n + 1) * q_rows)
            m_s[branch, s, pl.ds(out_off, q_rows), :] = m[rows]
            l_s[branch, s, pl.ds(out_off, q_rows), :] = l[rows]
            a_s[branch, s, pl.ds(out_off, q_rows), :] = num[rows]

    def super_tile(jt, _):
        l0 = jt * WINDOW_SUB

        def d16_body(r, _):
            branch_tile(0, [r], l0, 0)
            return 0
        lax.fori_loop(0, N_RES, d16_body, 0)

        def d4_body(t, _):
            c, u = t >> 2, t & 3
            branch_tile(1, [4 * a + c for a in range(4)], l0 + 32 * u, 32 * u)
            return 0
        lax.fori_loop(0, 16, d4_body, 0)

        def d1_body(u, _):
            branch_tile(2, list(range(N_RES)), l0 + 16 * u, 16 * u)
            return 0
        lax.fori_loop(0, 8, d1_body, 0)

        def merge(r, _):
            m0, m1, m2 = m_s[0, r], m_s[1, r], m_s[2, r]
            mm = jnp.maximum(jnp.maximum(m0, m1), m2)
            e0, e1, e2 = jnp.exp(m0 - mm), jnp.exp(m1 - mm), jnp.exp(m2 - mm)
            num = e0 * a_s[0, r] + e1 * a_s[1, r] + e2 * a_s[2, r]
            den = e0 * l_s[0, r] + e1 * l_s[1, r] + e2 * l_s[2, r]
            o_ref[0, 0, r, pl.ds(pl.multiple_of(l0, WINDOW_SUB), WINDOW_SUB), :] = (
                num / den).astype(BF16)
            return 0
        lax.fori_loop(0, N_RES, merge, 0)
        return 0

    lax.fori_loop(0, L // WINDOW_SUB, super_tile, 0)


def _dilated(qkv):
    B, _, R, L, _ = qkv.shape
    blk = (1, 1, R, L, LANES)
    scr = pltpu.VMEM((3, R, WINDOW_SUB, LANES), F32)
    return pl.pallas_call(
        _dil_kernel,
        out_shape=jax.ShapeDtypeStruct((B, N_PAIRS, R, L, LANES), BF16),
        grid=(B, N_PAIRS),
        in_specs=[
            pl.BlockSpec(blk, lambda b, p: (b, p, 0, 0, 0)),
            pl.BlockSpec(blk, lambda b, p: (b, N_PAIRS + p, 0, 0, 0)),
            pl.BlockSpec(blk, lambda b, p: (b, 2 * N_PAIRS + p, 0, 0, 0)),
        ],
        out_specs=pl.BlockSpec(blk, lambda b, p: (b, p, 0, 0, 0)),
        scratch_shapes=[
            pltpu.VMEM((2, 128, 256), F32), pltpu.VMEM((2, 128, 256), F32),
            pltpu.VMEM((2, 256, 512), F32), scr, scr, scr],
        compiler_params=pltpu.CompilerParams(
            dimension_semantics=("parallel", "parallel"),
            vmem_limit_bytes=VMEM_LIMIT),
        name="dilated",
    )(qkv, qkv, qkv)


def _layer_norm(y, g, b):
    mu = jnp.mean(y, axis=-1, keepdims=True)
    d = y - mu
    var = jnp.mean(d * d, axis=-1, keepdims=True)
    return d * lax.rsqrt(var + LN_EPS) * g + b


def _ffn_kernel(o_ref, x_ref, wo_ref, g1_ref, b1_ref, w1_ref, w2_ref, g2_ref, b2_ref, out_ref):
    o = jnp.concatenate([o_ref[0, j, 0] for j in range(N_PAIRS)], axis=1)
    mix = jnp.dot(o, wo_ref[...], preferred_element_type=F32)
    x1 = _layer_norm(ALPHA * x_ref[0] + mix, g1_ref[...], b1_ref[...])
    x1b = x1.astype(BF16)
    f = None
    for c in range(D_FF // D_MODEL):
        cs = slice(c * D_MODEL, (c + 1) * D_MODEL)
        h = jnp.dot(x1b, w1_ref[:, cs], preferred_element_type=F32)
        h = jnp.square(jnp.maximum(h, 0.0)).astype(BF16)
        fc = jnp.dot(h, w2_ref[cs, :], preferred_element_type=F32)
        f = fc if f is None else f + fc
    out_ref[0] = _layer_norm(ALPHA * x1 + f, g2_ref[...], b2_ref[...])


def _ffn_block(o, x, wo, g1, b1, w1, w2, g2, b2, tm=512):
    B, S, D = x.shape
    n_res, L = o.shape[2], o.shape[3]
    xv = x.reshape(B, L, n_res * D)
    vec = lambda t: t.reshape(1, D)
    row_spec = pl.BlockSpec((1, tm, D), lambda b, r, i: (b, i, r))
    out = pl.pallas_call(
        _ffn_kernel,
        out_shape=jax.ShapeDtypeStruct((B, L, n_res * D), F32),
        grid=(B, n_res, L // tm),
        in_specs=[
            pl.BlockSpec((1, N_PAIRS, 1, tm, LANES), lambda b, r, i: (b, 0, r, i, 0)),
            row_spec,
            _const_spec((D, D)), _const_spec((1, D)), _const_spec((1, D)),
            _const_spec((D, D_FF)), _const_spec((D_FF, D)),
            _const_spec((1, D)), _const_spec((1, D)),
        ],
        out_specs=row_spec,
        compiler_params=pltpu.CompilerParams(
            dimension_semantics=("parallel", "parallel", "parallel"),
            vmem_limit_bytes=VMEM_LIMIT),
        name="ffn_block",
    )(o, xv, wo, vec(g1), vec(b1), w1, w2, vec(g2), vec(b2))
    return out.reshape(B, S, D)


def _rotary_tables(S, n_res):
    half = ROPE_DIM // 2
    inv_freq = ROPE_THETA ** (-jnp.arange(half, dtype=F32) / half)
    pos = jnp.arange(S, dtype=jnp.int32).astype(F32)
    ang = pos[:, None] * inv_freq[None, :]
    cos, sin = jnp.cos(ang), jnp.sin(ang)
    ones = jnp.ones((S, HEAD_DIM - ROPE_DIM), F32)
    zeros8 = jnp.zeros((S, half), F32)
    zeros = jnp.zeros((S, HEAD_DIM - ROPE_DIM), F32)
    c = jnp.concatenate([cos, cos, ones], axis=1)
    sa = jnp.concatenate([-sin, zeros8, zeros], axis=1)
    sb = jnp.concatenate([zeros8, sin, zeros], axis=1)
    L = S // n_res

    def lay(t):
        t = jnp.concatenate([t, t], axis=1)
        return t.reshape(L, n_res, LANES).transpose(1, 0, 2)
    return lay(c), lay(sa), lay(sb)


def kernel(x, w_qkv_0, w_o_0, ln1_g_0, ln1_b_0, w_ff1_0, w_ff2_0, ln2_g_0, ln2_b_0,
           w_qkv_1, w_o_1, ln1_g_1, ln1_b_1, w_ff1_1, w_ff2_1, ln2_g_1, ln2_b_1):
    B, S, D = x.shape
    bf = lambda w: w.astype(BF16)

    qkv0 = _qkv_proj(x, bf(w_qkv_0), 1)
    o0 = _stick_breaking(qkv0)
    x = _ffn_block(o0, x, bf(w_o_0), ln1_g_0, ln1_b_0, bf(w_ff1_0), bf(w_ff2_0), ln2_g_0, ln2_b_0)

    qkv1 = _qkv_proj(x, bf(w_qkv_1), N_RES, tables=_rotary_tables(S, N_RES))
    o1 = _dilated(qkv1)
    x = _ffn_block(o1, x, bf(w_o_1), ln1_g_1, ln1_b_1, bf(w_ff1_1), bf(w_ff2_1), ln2_g_1, ln2_b_1)
    return x
```

```python
import functools
import math

import jax
import jax.numpy as jnp
from jax import lax
from jax.experimental import pallas as pl
from jax.experimental.pallas import tpu as pltpu

D_MODEL = 1024
N_HEADS = 16
HEAD_DIM = 64
D_FF = 4096
ROPE_THETA = 500000.0
ROPE_DIM = 16
DEPTH = 2
ALPHA = (2 * DEPTH) ** 0.25
LN_EPS = 1e-5
WINDOW_SUB = 128
N_RES = 16

LANES = 128
N_PAIRS = D_MODEL // LANES
N_COLS = 3 * N_PAIRS
VMEM_LIMIT = 56 * 1024 * 1024

LOG2E = 1.4426950408889634
SB_DONE = 160.0
SB_TQ = 128
SB_HK = 160
SB_PAD = 256

F32 = jnp.float32
BF16 = jnp.bfloat16


def _const_spec(shape):
    nd = len(shape)
    return pl.BlockSpec(shape, lambda *_: (0,) * nd, pipeline_mode=pl.Buffered(1))


def _qkv_kernel(*refs, rotary, q_scale):
    if rotary:
        x_ref, w_ref, c_ref, sa_ref, sb_ref, o_ref = refs
    else:
        x_ref, w_ref, o_ref = refs
    x = x_ref[0].astype(BF16)
    for part in range(3):
        y = jnp.dot(x, w_ref[:, part * D_MODEL:(part + 1) * D_MODEL],
                    preferred_element_type=F32)
        for j in range(N_PAIRS):
            col = y[:, j * LANES:(j + 1) * LANES]
            if rotary and part < 2:
                col = (col * c_ref[0, 0]
                       + pltpu.roll(col, LANES - ROPE_DIM // 2, 1) * sa_ref[0, 0]
                       + pltpu.roll(col, ROPE_DIM // 2, 1) * sb_ref[0, 0])
            if part == 0:
                col = col * q_scale
            o_ref[0, part * N_PAIRS + j, 0] = col.astype(BF16)


def _qkv_proj(x, w_bf16, n_res, q_scale, tables=None, tm=512):
    B, S, D = x.shape
    L = S // n_res
    xv = x.reshape(B, L, n_res * D)
    rotary = tables is not None
    in_specs = [
        pl.BlockSpec((1, tm, D), lambda b, r, i: (b, i, r)),
        _const_spec((D, 3 * D)),
    ]
    args = [xv, w_bf16]
    if rotary:
        for t in tables:
            in_specs.append(pl.BlockSpec((1, 1, tm, LANES), lambda b, r, i: (0, r, i, 0)))
            args.append(t.reshape(1, n_res, L, LANES))
    return pl.pallas_call(
        functools.partial(_qkv_kernel, rotary=rotary, q_scale=q_scale),
        out_shape=jax.ShapeDtypeStruct((B, N_COLS, n_res, L, LANES), BF16),
        grid=(B, n_res, L // tm),
        in_specs=in_specs,
        out_specs=pl.BlockSpec((1, N_COLS, 1, tm, LANES), lambda b, r, i: (b, 0, r, i, 0)),
        compiler_params=pltpu.CompilerParams(
            dimension_semantics=("parallel", "parallel", "parallel"),
            vmem_limit_bytes=VMEM_LIMIT),
        name="qkv_rot" if rotary else "qkv",
    )(*args)


def _softplus2(z):
    neg_abs = lax.bitcast_convert_type(
        lax.bitcast_convert_type(z, jnp.uint32) | jnp.uint32(0x80000000), F32)
    return jnp.maximum(z, 0.0) + jnp.log2(1.0 + jnp.exp2(neg_abs))


def _sb_kernel(q_ref, k_ref, v_ref, o_ref, kp_ref, vp_ref, acc_ref, c_ref, *, group):
    S = q_ref.shape[3]
    tq, hk, pad = SB_TQ, SB_HK, SB_PAD
    nblk = S // tq

    kp_ref[pl.ds(0, pad), :] = jnp.zeros((pad, LANES), BF16)
    vp_ref[pl.ds(0, pad), :] = jnp.zeros((pad, LANES), BF16)

    def copy_in(i, _):
        r0 = pl.multiple_of(i * 512, 512)
        kp_ref[pl.ds(pad + r0, 512), :] = k_ref[0, 0, 0, pl.ds(r0, 512), :]
        vp_ref[pl.ds(pad + r0, 512), :] = v_ref[0, 0, 0, pl.ds(r0, 512), :]
        return 0
    lax.fori_loop(0, S // 512, copy_in, 0)

    lane = lax.broadcasted_iota(jnp.int32, (1, LANES), 1)
    head0 = lane < HEAD_DIM
    key_i = lax.broadcasted_iota(jnp.int32, (tq, 2 * tq), 0)
    qry_i = lax.broadcasted_iota(jnp.int32, (tq, 2 * tq), 1) & (tq - 1)
    causal = key_i < qry_i

    def suffix_ones(n):
        return (lax.broadcasted_iota(jnp.int32, (n, n), 1)
                >= lax.broadcasted_iota(jnp.int32, (n, n), 0)).astype(BF16)
    t_diag, t_hist = suffix_ones(tq), suffix_ones(hk)
    dim_head0 = lax.broadcasted_iota(jnp.int32, (LANES, tq), 0) < HEAD_DIM

    def q_stack(q0):
        q = q_ref[0, 0, 0, pl.ds(q0, tq), :]
        zero = jnp.zeros_like(q)
        return jnp.concatenate([jnp.where(head0, q, zero), jnp.where(head0, zero, q)], axis=0)

    def scores(qs, k0, n):
        kc = kp_ref[pl.ds(k0, n), :]
        return lax.dot_general(kc, qs, (((1,), (1,)), ((), ())), preferred_element_type=F32)

    def weighted_values(k0, n, w):
        vc = vp_ref[pl.ds(k0, n), :]
        return lax.dot_general(vc, w.astype(BF16), (((0,), (0,)), ((), ())),
                               preferred_element_type=F32)

    def chunk(qs, k0, n, t_mat, carry, mask):
        z = scores(qs, k0, n)
        sp = _softplus2(z)
        if mask is not None:
            sp = jnp.where(mask, sp, 0.0)
        r = jnp.dot(t_mat, sp.astype(BF16), preferred_element_type=F32)
        logw = z - r if carry is None else z - r - carry
        w = jnp.exp2(logw)
        if mask is not None:
            w = jnp.where(mask, w, 0.0)
        return weighted_values(k0, n, w), r[0:1, :]

    def fast(blks):
        n = len(blks)
        kd = [pl.multiple_of(pad + b * tq, tq) for b in blks]
        kh = [pl.multiple_of(pad + b * tq - hk, 32) for b in blks]
        qs = [q_stack(pl.multiple_of(b * tq, tq)) for b in blks]
        zd = [scores(qs[g], kd[g], tq) for g in range(n)]
        zh = [scores(qs[g], kh[g], hk) for g in range(n)]
        spd = [jnp.where(causal, _softplus2(z), 0.0) for z in zd]
        sph = [_softplus2(z) for z in zh]
        rd = [jnp.dot(t_diag, s.astype(BF16), preferred_element_type=F32) for s in spd]
        rh = [jnp.dot(t_hist, s.astype(BF16), preferred_element_type=F32) for s in sph]
        cd = [jnp.sum(s, axis=0, keepdims=True) for s in spd]
        wd = [jnp.where(causal, jnp.exp2(zd[g] - rd[g]), 0.0) for g in range(n)]
        wh = [jnp.exp2(zh[g] - rh[g] - cd[g]) for g in range(n)]
        cmins = []
        for g in range(n):
            acc_ref[g] = weighted_values(kd[g], tq, wd[g]) + weighted_values(kh[g], hk, wh[g])
            c = cd[g] + rh[g][0:1, :]
            c_ref[g] = c
            cmins.append(jnp.min(c))
        return cmins

    def earlier_keys(g, blk, cmin):
        q0 = pl.multiple_of(blk * tq, tq)

        def cond(carry):
            hi, cmin = carry
            return jnp.logical_and(hi > 0, cmin < SB_DONE)

        def body(carry):
            hi, _ = carry
            qs = q_stack(q0)
            k0 = pl.multiple_of(pad + hi - tq, 32)
            c = c_ref[g]
            acc, c_n = chunk(qs, k0, tq, t_diag, c, None)
            acc_ref[g] += acc
            c = c + c_n
            c_ref[g] = c
            return hi - tq, jnp.min(c)

        lax.while_loop(cond, body, (q0 - hk, cmin))

    def write_out(g, blk):
        q0 = pl.multiple_of(blk * tq, tq)
        acc = acc_ref[g]
        o_t = jnp.where(dim_head0, acc[:, :tq], acc[:, tq:])
        o_ref[0, 0, 0, pl.ds(q0, tq), :] = o_t.T.astype(BF16)

    def block_group(i, _):
        blks = [i * group + g for g in range(group)]
        cmins = fast(blks)
        for g in range(group):
            earlier_keys(g, blks[g], cmins[g])
        for g in range(group):
            write_out(g, blks[g])
        return 0

    lax.fori_loop(0, nblk // group, block_group, 0)


def _stick_breaking(qkv, group=8):
    B, _, _, S, _ = qkv.shape
    assert S % (SB_TQ * group) == 0 and S % 512 == 0
    blk = (1, 1, 1, S, LANES)
    return pl.pallas_call(
        functools.partial(_sb_kernel, group=group),
        out_shape=jax.ShapeDtypeStruct((B, N_PAIRS, 1, S, LANES), BF16),
        grid=(B, N_PAIRS),
        in_specs=[
            pl.BlockSpec(blk, lambda b, p: (b, p, 0, 0, 0)),
            pl.BlockSpec(blk, lambda b, p: (b, N_PAIRS + p, 0, 0, 0)),
            pl.BlockSpec(blk, lambda b, p: (b, 2 * N_PAIRS + p, 0, 0, 0)),
        ],
        out_specs=pl.BlockSpec(blk, lambda b, p: (b, p, 0, 0, 0)),
        scratch_shapes=[
            pltpu.VMEM((SB_PAD + S, LANES), BF16), pltpu.VMEM((SB_PAD + S, LANES), BF16),
            pltpu.VMEM((group, LANES, 2 * SB_TQ), F32), pltpu.VMEM((group, 1, 2 * SB_TQ), F32)],
        compiler_params=pltpu.CompilerParams(
            dimension_semantics=("parallel", "parallel"),
            vmem_limit_bytes=VMEM_LIMIT),
        name="stick_breaking",
    )(qkv, qkv, qkv)


_DIL_TILES = ((1, 128), (4, 32), (16, 16))


def _dil_bias(n_slab, q_rows, first):
    tq, tk = n_slab * q_rows, 2 * n_slab * q_rows
    qi = lax.broadcasted_iota(jnp.int32, (tq, tk), 0)
    kj = lax.broadcasted_iota(jnp.int32, (tq, tk), 1)
    sh = q_rows.bit_length() - 1
    a, i = qi >> sh, qi & (q_rows - 1)
    a2, j = kj >> (sh + 1), kj & (2 * q_rows - 1)
    dq = 0 if first else q_rows
    dist = n_slab * (dq + i - j) + (a - a2)
    ok = jnp.logical_and(dist >= 0, dist <= WINDOW_SUB)
    return jnp.where(ok, 0.0, -jnp.inf).astype(F32)


def _dil_kernel(q_ref, k_ref, v_ref, o_ref, b16_ref, b4_ref, b1_ref, m_s, l_s, a_s):
    L = q_ref.shape[3]
    lane = lax.broadcasted_iota(jnp.int32, (1, LANES), 1)
    head0 = lane < HEAD_DIM
    bias_refs = (b16_ref, b4_ref, b1_ref)
    for (n_slab, q_rows), b_ref in zip(_DIL_TILES, bias_refs):
        b_ref[0] = _dil_bias(n_slab, q_rows, False)
        b_ref[1] = _dil_bias(n_slab, q_rows, True)

    def tile(q, k, v, bias):
        ms, ls, num = [], [], None
        for h in range(2):
            hmask = head0 if h == 0 else jnp.logical_not(head0)
            qh = jnp.where(hmask, q, jnp.zeros_like(q))
            s = lax.dot_general(qh, k, (((1,), (1,)), ((), ())),
                                preferred_element_type=F32) + bias
            m = jnp.max(s, axis=1, keepdims=True)
            p = jnp.exp(s - m)
            ls.append(jnp.sum(p, axis=1, keepdims=True))
            ms.append(m)
            vh = jnp.where(hmask, v, jnp.zeros_like(v))
            pv = jnp.dot(p.astype(BF16), vh, preferred_element_type=F32)
            num = pv if num is None else num + pv
        return jnp.where(head0, ms[0], ms[1]), jnp.where(head0, ls[0], ls[1]), num

    def branch_tile(branch, slabs, lq0, out_off):
        n_slab, q_rows = _DIL_TILES[branch]
        lq0 = pl.multiple_of(lq0, q_rows)
        first = lq0 < q_rows
        kq0 = pl.multiple_of(jnp.maximum(lq0 - q_rows, 0), q_rows)
        q = jnp.concatenate([q_ref[0, 0, s, pl.ds(lq0, q_rows), :] for s in slabs], axis=0)
        k = jnp.concatenate([k_ref[0, 0, s, pl.ds(kq0, 2 * q_rows), :] for s in slabs], axis=0)
        v = jnp.concatenate([v_ref[0, 0, s, pl.ds(kq0, 2 * q_rows), :] for s in slabs], axis=0)
        bias = bias_refs[branch][first.astype(jnp.int32)]
        m, l, num = tile(q, k, v, bias)
        out_off = pl.multiple_of(out_off, q_rows)
        for n, s in enumerate(slabs):
            rows = slice(n * q_rows, (n + 1) * q_rows)
            m_s[branch, s, pl.ds(out_off, q_rows), :] = m[rows]
            l_s[branch, s, pl.ds(out_off, q_rows), :] = l[rows]
            a_s[branch, s, pl.ds(out_off, q_rows), :] = num[rows]

    def super_tile(jt, _):
        l0 = jt * WINDOW_SUB

        def d16_body(r, _):
            branch_tile(0, [r], l0, 0)
            return 0
        lax.fori_loop(0, N_RES, d16_body, 0)

        def d4_body(t, _):
            c, u = t >> 2, t & 3
            branch_tile(1, [4 * a + c for a in range(4)], l0 + 32 * u, 32 * u)
            return 0
        lax.fori_loop(0, 16, d4_body, 0)

        def d1_body(u, _):
            branch_tile(2, list(range(N_RES)), l0 + 16 * u, 16 * u)
            return 0
        lax.fori_loop(0, 8, d1_body, 0)

        def merge(r, _):
            m0, m1, m2 = m_s[0, r], m_s[1, r], m_s[2, r]
            mm = jnp.maximum(jnp.maximum(m0, m1), m2)
            e0, e1, e2 = jnp.exp(m0 - mm), jnp.exp(m1 - mm), jnp.exp(m2 - mm)
            num = e0 * a_s[0, r] + e1 * a_s[1, r] + e2 * a_s[2, r]
            den = e0 * l_s[0, r] + e1 * l_s[1, r] + e2 * l_s[2, r]
            o_ref[0, 0, r, pl.ds(pl.multiple_of(l0, WINDOW_SUB), WINDOW_SUB), :] = (
                num / den).astype(BF16)
            return 0
        lax.fori_loop(0, N_RES, merge, 0)
        return 0

    lax.fori_loop(0, L // WINDOW_SUB, super_tile, 0)


def _dilated(qkv):
    B, _, R, L, _ = qkv.shape
    blk = (1, 1, R, L, LANES)
    scr = pltpu.VMEM((3, R, WINDOW_SUB, LANES), F32)
    return pl.pallas_call(
        _dil_kernel,
        out_shape=jax.ShapeDtypeStruct((B, N_PAIRS, R, L, LANES), BF16),
        grid=(B, N_PAIRS),
        in_specs=[
            pl.BlockSpec(blk, lambda b, p: (b, p, 0, 0, 0)),
            pl.BlockSpec(blk, lambda b, p: (b, N_PAIRS + p, 0, 0, 0)),
            pl.BlockSpec(blk, lambda b, p: (b, 2 * N_PAIRS + p, 0, 0, 0)),
        ],
        out_specs=pl.BlockSpec(blk, lambda b, p: (b, p, 0, 0, 0)),
        scratch_shapes=[
            pltpu.VMEM((2, 128, 256), F32), pltpu.VMEM((2, 128, 256), F32),
            pltpu.VMEM((2, 256, 512), F32), scr, scr, scr],
        compiler_params=pltpu.CompilerParams(
            dimension_semantics=("parallel", "parallel"),
            vmem_limit_bytes=VMEM_LIMIT),
        name="dilated",
    )(qkv, qkv, qkv)


def _layer_norm(y, g, b):
    mu = jnp.mean(y, axis=-1, keepdims=True)
    d = y - mu
    var = jnp.mean(d * d, axis=-1, keepdims=True)
    return d * lax.rsqrt(var + LN_EPS) * g + b


def _ffn_kernel(o_ref, x_ref, wo_ref, g1_ref, b1_ref, w1_ref, w2_ref, g2_ref, b2_ref, out_ref):
    o = jnp.concatenate([o_ref[0, j, 0] for j in range(N_PAIRS)], axis=1)
    mix = jnp.dot(o, wo_ref[...], preferred_element_type=F32)
    x1 = _layer_norm(ALPHA * x_ref[0] + mix, g1_ref[...], b1_ref[...])
    x1b = x1.astype(BF16)
    f = None
    for c in range(D_FF // D_MODEL):
        cs = slice(c * D_MODEL, (c + 1) * D_MODEL)
        h = jnp.dot(x1b, w1_ref[:, cs], preferred_element_type=F32)
        h = jnp.square(jnp.maximum(h, 0.0)).astype(BF16)
        fc = jnp.dot(h, w2_ref[cs, :], preferred_element_type=F32)
        f = fc if f is None else f + fc
    out_ref[0] = _layer_norm(ALPHA * x1 + f, g2_ref[...], b2_ref[...])


def _ffn_block(o, x, wo, g1, b1, w1, w2, g2, b2, tm=512):
    B, S, D = x.shape
    n_res, L = o.shape[2], o.shape[3]
    xv = x.reshape(B, L, n_res * D)
    vec = lambda t: t.reshape(1, D)
    row_spec = pl.BlockSpec((1, tm, D), lambda b, r, i: (b, i, r))
    out = pl.pallas_call(
        _ffn_kernel,
        out_shape=jax.ShapeDtypeStruct((B, L, n_res * D), F32),
        grid=(B, n_res, L // tm),
        in_specs=[
            pl.BlockSpec((1, N_PAIRS, 1, tm, LANES), lambda b, r, i: (b, 0, r, i, 0)),
            row_spec,
            _const_spec((D, D)), _const_spec((1, D)), _const_spec((1, D)),
            _const_spec((D, D_FF)), _const_spec((D_FF, D)),
            _const_spec((1, D)), _const_spec((1, D)),
        ],
        out_specs=row_spec,
        compiler_params=pltpu.CompilerParams(
            dimension_semantics=("parallel", "parallel", "parallel"),
            vmem_limit_bytes=VMEM_LIMIT),
        name="ffn_block",
    )(o, xv, wo, vec(g1), vec(b1), w1, w2, vec(g2), vec(b2))
    return out.reshape(B, S, D)


def _rotary_tables(S, n_res):
    half = ROPE_DIM // 2
    inv_freq = ROPE_THETA ** (-jnp.arange(half, dtype=F32) / half)
    pos = jnp.arange(S, dtype=jnp.int32).astype(F32)
    ang = pos[:, None] * inv_freq[None, :]
    cos, sin = jnp.cos(ang), jnp.sin(ang)
    ones = jnp.ones((S, HEAD_DIM - ROPE_DIM), F32)
    zeros8 = jnp.zeros((S, half), F32)
    zeros = jnp.zeros((S, HEAD_DIM - ROPE_DIM), F32)
    c = jnp.concatenate([cos, cos, ones], axis=1)
    sa = jnp.concatenate([-sin, zeros8, zeros], axis=1)
    sb = jnp.concatenate([zeros8, sin, zeros], axis=1)
    L = S // n_res

    def lay(t):
        t = jnp.concatenate([t, t], axis=1)
        return t.reshape(L, n_res, LANES).transpose(1, 0, 2)
    return lay(c), lay(sa), lay(sb)


def kernel(x, w_qkv_0, w_o_0, ln1_g_0, ln1_b_0, w_ff1_0, w_ff2_0, ln2_g_0, ln2_b_0,
           w_qkv_1, w_o_1, ln1_g_1, ln1_b_1, w_ff1_1, w_ff2_1, ln2_g_1, ln2_b_1):
    B, S, D = x.shape
    bf = lambda w: w.astype(BF16)

    qkv0 = _qkv_proj(x, bf(w_qkv_0), 1, LOG2E / math.sqrt(HEAD_DIM))
    o0 = _stick_breaking(qkv0)
    x = _ffn_block(o0, x, bf(w_o_0), ln1_g_0, ln1_b_0, bf(w_ff1_0), bf(w_ff2_0), ln2_g_0, ln2_b_0)

    qkv1 = _qkv_proj(x, bf(w_qkv_1), N_RES, 1.0 / math.sqrt(HEAD_DIM), tables=_rotary_tables(S, N_RES))
    o1 = _dilated(qkv1)
    x = _ffn_block(o1, x, bf(w_o_1), ln1_g_1, ln1_b_1, bf(w_ff1_1), bf(w_ff2_1), ln2_g_1, ln2_b_1)
    return x
```

```python
import functools
import math

import jax
import jax.numpy as jnp
from jax import lax
from jax.experimental import pallas as pl
from jax.experimental.pallas import tpu as pltpu

D_MODEL = 1024
N_HEADS = 16
HEAD_DIM = 64
D_FF = 4096
ROPE_THETA = 500000.0
ROPE_DIM = 16
DEPTH = 2
ALPHA = (2 * DEPTH) ** 0.25
LN_EPS = 1e-5
WINDOW_SUB = 128
N_RES = 16

LANES = 128
N_PAIRS = D_MODEL // LANES
N_COLS = 3 * N_PAIRS
VMEM_LIMIT = 56 * 1024 * 1024

LOG2E = 1.4426950408889634
SB_DONE = 160.0
SB_TQ = 128
SB_HK = 160
SB_PAD = 256

F32 = jnp.float32
BF16 = jnp.bfloat16


def _const_spec(shape):
    nd = len(shape)
    return pl.BlockSpec(shape, lambda *_: (0,) * nd, pipeline_mode=pl.Buffered(1))


def _qkv_kernel(*refs, rotary, q_scale):
    if rotary:
        x_ref, w_ref, c_ref, sa_ref, sb_ref, o_ref = refs
    else:
        x_ref, w_ref, o_ref = refs
    x = x_ref[0].astype(BF16)
    for part in range(3):
        y = jnp.dot(x, w_ref[:, part * D_MODEL:(part + 1) * D_MODEL],
                    preferred_element_type=F32)
        for j in range(N_PAIRS):
            col = y[:, j * LANES:(j + 1) * LANES]
            if rotary and part < 2:
                col = (col * c_ref[0, 0]
                       + pltpu.roll(col, LANES - ROPE_DIM // 2, 1) * sa_ref[0, 0]
                       + pltpu.roll(col, ROPE_DIM // 2, 1) * sb_ref[0, 0])
            if part == 0:
                col = col * q_scale
            o_ref[0, part * N_PAIRS + j, 0] = col.astype(BF16)


def _qkv_proj(x, w_bf16, n_res, q_scale, tables=None, tm=512):
    B, S, D = x.shape
    L = S // n_res
    xv = x.reshape(B, L, n_res * D)
    rotary = tables is not None
    in_specs = [
        pl.BlockSpec((1, tm, D), lambda b, r, i: (b, i, r)),
        _const_spec((D, 3 * D)),
    ]
    args = [xv, w_bf16]
    if rotary:
        for t in tables:
            in_specs.append(pl.BlockSpec((1, 1, tm, LANES), lambda b, r, i: (0, r, i, 0)))
            args.append(t.reshape(1, n_res, L, LANES))
    return pl.pallas_call(
        functools.partial(_qkv_kernel, rotary=rotary, q_scale=q_scale),
        out_shape=jax.ShapeDtypeStruct((B, N_COLS, n_res, L, LANES), BF16),
        grid=(B, n_res, L // tm),
        in_specs=in_specs,
        out_specs=pl.BlockSpec((1, N_COLS, 1, tm, LANES), lambda b, r, i: (b, 0, r, i, 0)),
        compiler_params=pltpu.CompilerParams(
            dimension_semantics=("parallel", "parallel", "parallel"),
            vmem_limit_bytes=VMEM_LIMIT),
        name="qkv_rot" if rotary else "qkv",
    )(*args)


def _softplus2(z):
    neg_abs = lax.bitcast_convert_type(
        lax.bitcast_convert_type(z, jnp.uint32) | jnp.uint32(0x80000000), F32)
    return jnp.maximum(z, 0.0) + jnp.log2(1.0 + jnp.exp2(neg_abs))


def _sb_kernel(q_ref, k_ref, v_ref, o_ref, kp_ref, vp_ref, acc_ref, c_ref, *, group):
    S = q_ref.shape[3]
    tq, hk, pad = SB_TQ, SB_HK, SB_PAD
    nblk = S // tq

    kp_ref[pl.ds(0, pad), :] = jnp.zeros((pad, LANES), BF16)
    vp_ref[pl.ds(0, pad), :] = jnp.zeros((pad, LANES), BF16)

    def copy_in(i, _):
        r0 = pl.multiple_of(i * 512, 512)
        kp_ref[pl.ds(pad + r0, 512), :] = k_ref[0, 0, 0, pl.ds(r0, 512), :]
        vp_ref[pl.ds(pad + r0, 512), :] = v_ref[0, 0, 0, pl.ds(r0, 512), :]
        return 0
    lax.fori_loop(0, S // 512, copy_in, 0)

    lane = lax.broadcasted_iota(jnp.int32, (1, LANES), 1)
    head0 = lane < HEAD_DIM
    key_i = lax.broadcasted_iota(jnp.int32, (tq, 2 * tq), 0)
    qry_i = lax.broadcasted_iota(jnp.int32, (tq, 2 * tq), 1) & (tq - 1)
    causal = key_i < qry_i

    def suffix_ones(n):
        return (lax.broadcasted_iota(jnp.int32, (n, n), 1)
                >= lax.broadcasted_iota(jnp.int32, (n, n), 0)).astype(BF16)
    t_diag, t_hist = suffix_ones(tq), suffix_ones(hk)
    dim_head0 = lax.broadcasted_iota(jnp.int32, (LANES, tq), 0) < HEAD_DIM

    def q_stack(q0):
        q = q_ref[0, 0, 0, pl.ds(q0, tq), :]
        zero = jnp.zeros_like(q)
        return jnp.concatenate([jnp.where(head0, q, zero), jnp.where(head0, zero, q)], axis=0)

    def scores(qs, k0, n):
        kc = kp_ref[pl.ds(k0, n), :]
        return lax.dot_general(kc, qs, (((1,), (1,)), ((), ())), preferred_element_type=F32)

    def weighted_values(k0, n, w):
        vc = vp_ref[pl.ds(k0, n), :]
        return lax.dot_general(vc, w.astype(BF16), (((0,), (0,)), ((), ())),
                               preferred_element_type=F32)

    def chunk(qs, k0, n, t_mat, carry, mask):
        z = scores(qs, k0, n)
        sp = _softplus2(z)
        if mask is not None:
            sp = jnp.where(mask, sp, 0.0)
        r = jnp.dot(t_mat, sp.astype(BF16), preferred_element_type=F32)
        logw = z - r if carry is None else z - r - carry
        w = jnp.exp2(logw)
        if mask is not None:
            w = jnp.where(mask, w, 0.0)
        return weighted_values(k0, n, w), r[0:1, :]

    def fast(blks):
        n = len(blks)
        kd = [pl.multiple_of(pad + b * tq, tq) for b in blks]
        kh = [pl.multiple_of(pad + b * tq - hk, 32) for b in blks]
        qs = [q_stack(pl.multiple_of(b * tq, tq)) for b in blks]
        zd = [scores(qs[g], kd[g], tq) for g in range(n)]
        zh = [scores(qs[g], kh[g], hk) for g in range(n)]
        spd = [jnp.where(causal, _softplus2(z), 0.0) for z in zd]
        sph = [_softplus2(z) for z in zh]
        rd = [jnp.dot(t_diag, s.astype(BF16), preferred_element_type=F32) for s in spd]
        rh = [jnp.dot(t_hist, s.astype(BF16), preferred_element_type=F32) for s in sph]
        cd = [jnp.sum(s, axis=0, keepdims=True) for s in spd]
        wd = [jnp.where(causal, jnp.exp2(zd[g] - rd[g]), 0.0) for g in range(n)]
        wh = [jnp.exp2(zh[g] - rh[g] - cd[g]) for g in range(n)]
        cmins = []
        for g in range(n):
            acc_ref[g] = weighted_values(kd[g], tq, wd[g]) + weighted_values(kh[g], hk, wh[g])
            c = cd[g] + rh[g][0:1, :]
            c_ref[g] = c
            cmins.append(jnp.min(c))
        return cmins

    def earlier_keys(g, blk, cmin):
        q0 = pl.multiple_of(blk * tq, tq)

        def cond(carry):
            hi, cmin = carry
            return jnp.logical_and(hi > 0, cmin < SB_DONE)

        def body(carry):
            hi, _ = carry
            qs = q_stack(q0)
            k0 = pl.multiple_of(pad + hi - tq, 32)
            c = c_ref[g]
            acc, c_n = chunk(qs, k0, tq, t_diag, c, None)
            acc_ref[g] += acc
            c = c + c_n
            c_ref[g] = c
            return hi - tq, jnp.min(c)

        lax.while_loop(cond, body, (q0 - hk, cmin))

    def write_out(g, blk):
        q0 = pl.multiple_of(blk * tq, tq)
        acc = acc_ref[g]
        o_t = jnp.where(dim_head0, acc[:, :tq], acc[:, tq:])
        o_ref[0, 0, 0, pl.ds(q0, tq), :] = o_t.T.astype(BF16)

    def block_group(i, _):
        blks = [i * group + g for g in range(group)]
        cmins = fast(blks)
        for g in range(group):
            earlier_keys(g, blks[g], cmins[g])
        for g in range(group):
            write_out(g, blks[g])
        return 0

    lax.fori_loop(0, nblk // group, block_group, 0)


def _stick_breaking(qkv, group=8):
    B, _, _, S, _ = qkv.shape
    assert S % (SB_TQ * group) == 0 and S % 512 == 0
    blk = (1, 1, 1, S, LANES)
    return pl.pallas_call(
        functools.partial(_sb_kernel, group=group),
        out_shape=jax.ShapeDtypeStruct((B, N_PAIRS, 1, S, LANES), BF16),
        grid=(B, N_PAIRS),
        in_specs=[
            pl.BlockSpec(blk, lambda b, p: (b, p, 0, 0, 0)),
            pl.BlockSpec(blk, lambda b, p: (b, N_PAIRS + p, 0, 0, 0)),
            pl.BlockSpec(blk, lambda b, p: (b, 2 * N_PAIRS + p, 0, 0, 0)),
        ],
        out_specs=pl.BlockSpec(blk, lambda b, p: (b, p, 0, 0, 0)),
        scratch_shapes=[
            pltpu.VMEM((SB_PAD + S, LANES), BF16), pltpu.VMEM((SB_PAD + S, LANES), BF16),
            pltpu.VMEM((group, LANES, 2 * SB_TQ), F32), pltpu.VMEM((group, 1, 2 * SB_TQ), F32)],
        compiler_params=pltpu.CompilerParams(
            dimension_semantics=("parallel", "parallel"),
            vmem_limit_bytes=VMEM_LIMIT),
        name="stick_breaking",
    )(qkv, qkv, qkv)


_DIL_TILES = ((1, 128), (4, 32), (16, 16))


def _dil_bias(n_slab, q_rows, first):
    tq, tk = n_slab * q_rows, 2 * n_slab * q_rows
    kj = lax.broadcasted_iota(jnp.int32, (tk, 2 * tq), 0)
    qi = lax.broadcasted_iota(jnp.int32, (tk, 2 * tq), 1) & (tq - 1)
    sh = q_rows.bit_length() - 1
    a, i = qi >> sh, qi & (q_rows - 1)
    a2, j = kj >> (sh + 1), kj & (2 * q_rows - 1)
    dq = 0 if first else q_rows
    dist = n_slab * (dq + i - j) + (a - a2)
    ok = jnp.logical_and(dist >= 0, dist <= WINDOW_SUB)
    return jnp.where(ok, 0.0, -jnp.inf).astype(F32)


def _dil_kernel(q_ref, k_ref, v_ref, o_ref, b16_ref, b4_ref, b1_ref, lw_s, o_s):
    L = q_ref.shape[3]
    lane = lax.broadcasted_iota(jnp.int32, (1, LANES), 1)
    head0 = lane < HEAD_DIM
    bias_refs = (b16_ref, b4_ref, b1_ref)
    for (n_slab, q_rows), b_ref in zip(_DIL_TILES, bias_refs):
        b_ref[0] = _dil_bias(n_slab, q_rows, False)
        b_ref[1] = _dil_bias(n_slab, q_rows, True)

    def q_stack(q):
        zero = jnp.zeros_like(q)
        return jnp.concatenate([jnp.where(head0, q, zero), jnp.where(head0, zero, q)], axis=0)

    def tiles(branch, specs, lag):
        n_slab, q_rows = _DIL_TILES[branch]
        tq = n_slab * q_rows
        dim_head0 = lax.broadcasted_iota(jnp.int32, (LANES, tq), 0) < HEAD_DIM

        def scores(spec):
            slabs, lq0, _ = spec
            lq0 = pl.multiple_of(lq0, q_rows)
            first = lq0 < q_rows
            kq0 = pl.multiple_of(jnp.maximum(lq0 - q_rows, 0), q_rows)
            q = jnp.concatenate([q_ref[0, 0, s, pl.ds(lq0, q_rows), :] for s in slabs], axis=0)
            k = jnp.concatenate([k_ref[0, 0, s, pl.ds(kq0, 2 * q_rows), :] for s in slabs], axis=0)
            bias = bias_refs[branch][first.astype(jnp.int32)]
            s = lax.dot_general(k, q_stack(q), (((1,), (1,)), ((), ())),
                                preferred_element_type=F32) + bias
            return s, kq0

        def finish(spec, s, kq0):
            slabs, _, out_off = spec
            v = jnp.concatenate([v_ref[0, 0, sl, pl.ds(kq0, 2 * q_rows), :] for sl in slabs], axis=0)
            m = jnp.max(s, axis=0, keepdims=True)
            p = jnp.exp2(s - m)
            l = jnp.sum(p, axis=0, keepdims=True)
            num = lax.dot_general(v, p.astype(BF16), (((0,), (0,)), ((), ())),
                                  preferred_element_type=F32)
            num = num * (1.0 / l)
            lw = jnp.broadcast_to(m + jnp.log2(l), (LANES, 2 * tq))
            o_t = jnp.where(dim_head0, num[:, :tq], num[:, tq:]).T
            lw_t = jnp.where(dim_head0, lw[:, :tq], lw[:, tq:]).T
            out_off = pl.multiple_of(out_off, q_rows)
            for n, sl in enumerate(slabs):
                rows = slice(n * q_rows, (n + 1) * q_rows)
                lw_s[branch, sl, pl.ds(out_off, q_rows), :] = lw_t[rows]
                o_s[branch, sl, pl.ds(out_off, q_rows), :] = o_t[rows]

        pending = []
        for t in range(len(specs) + lag):
            if t < len(specs):
                pending.append(scores(specs[t]))
            if t >= lag:
                finish(specs[t - lag], *pending[t - lag])

    def super_tile(jt, _):
        l0 = jt * WINDOW_SUB
        tiles(0, [([r], l0, 0) for r in range(N_RES)], 3)
        tiles(1, [([4 * a + c for a in range(4)], l0 + 32 * u, 32 * u)
                  for c in range(4) for u in range(4)], 3)
        tiles(2, [(list(range(N_RES)), l0 + 16 * u, 16 * u) for u in range(8)], 1)
        for r in range(N_RES):
            w0, w1, w2 = lw_s[0, r], lw_s[1, r], lw_s[2, r]
            wm = jnp.maximum(jnp.maximum(w0, w1), w2)
            e0, e1, e2 = jnp.exp2(w0 - wm), jnp.exp2(w1 - wm), jnp.exp2(w2 - wm)
            num = e0 * o_s[0, r] + e1 * o_s[1, r] + e2 * o_s[2, r]
            o_ref[0, 0, r, pl.ds(pl.multiple_of(l0, WINDOW_SUB), WINDOW_SUB), :] = (
                num / (e0 + e1 + e2)).astype(BF16)
        return 0

    lax.fori_loop(0, L // WINDOW_SUB, super_tile, 0)


def _dilated(qkv):
    B, _, R, L, _ = qkv.shape
    blk = (1, 1, R, L, LANES)
    scr = pltpu.VMEM((3, R, WINDOW_SUB, LANES), F32)
    return pl.pallas_call(
        _dil_kernel,
        out_shape=jax.ShapeDtypeStruct((B, N_PAIRS, R, L, LANES), BF16),
        grid=(B, N_PAIRS),
        in_specs=[
            pl.BlockSpec(blk, lambda b, p: (b, p, 0, 0, 0)),
            pl.BlockSpec(blk, lambda b, p: (b, N_PAIRS + p, 0, 0, 0)),
            pl.BlockSpec(blk, lambda b, p: (b, 2 * N_PAIRS + p, 0, 0, 0)),
        ],
        out_specs=pl.BlockSpec(blk, lambda b, p: (b, p, 0, 0, 0)),
        scratch_shapes=[
            pltpu.VMEM((2, 256, 256), F32), pltpu.VMEM((2, 256, 256), F32),
            pltpu.VMEM((2, 512, 512), F32), scr, scr],
        compiler_params=pltpu.CompilerParams(
            dimension_semantics=("parallel", "parallel"),
            vmem_limit_bytes=VMEM_LIMIT),
        name="dilated",
    )(qkv, qkv, qkv)


def _layer_norm(y, g, b):
    mu = jnp.mean(y, axis=-1, keepdims=True)
    d = y - mu
    var = jnp.mean(d * d, axis=-1, keepdims=True)
    return d * lax.rsqrt(var + LN_EPS) * g + b


def _ffn_kernel(o_ref, x_ref, wo_ref, g1_ref, b1_ref, w1_ref, w2_ref, g2_ref, b2_ref, out_ref):
    o = jnp.concatenate([o_ref[0, j, 0] for j in range(N_PAIRS)], axis=1)
    mix = jnp.dot(o, wo_ref[...], preferred_element_type=F32)
    x1 = _layer_norm(ALPHA * x_ref[0] + mix, g1_ref[...], b1_ref[...])
    x1b = x1.astype(BF16)
    f = None
    for c in range(D_FF // D_MODEL):
        cs = slice(c * D_MODEL, (c + 1) * D_MODEL)
        h = jnp.dot(x1b, w1_ref[:, cs], preferred_element_type=F32)
        h = jnp.square(jnp.maximum(h, 0.0)).astype(BF16)
        fc = jnp.dot(h, w2_ref[cs, :], preferred_element_type=F32)
        f = fc if f is None else f + fc
    out_ref[0] = _layer_norm(ALPHA * x1 + f, g2_ref[...], b2_ref[...])


def _ffn_block(o, x, wo, g1, b1, w1, w2, g2, b2, tm=512):
    B, S, D = x.shape
    n_res, L = o.shape[2], o.shape[3]
    xv = x.reshape(B, L, n_res * D)
    vec = lambda t: t.reshape(1, D)
    row_spec = pl.BlockSpec((1, tm, D), lambda b, r, i: (b, i, r))
    out = pl.pallas_call(
        _ffn_kernel,
        out_shape=jax.ShapeDtypeStruct((B, L, n_res * D), F32),
        grid=(B, n_res, L // tm),
        in_specs=[
            pl.BlockSpec((1, N_PAIRS, 1, tm, LANES), lambda b, r, i: (b, 0, r, i, 0)),
            row_spec,
            _const_spec((D, D)), _const_spec((1, D)), _const_spec((1, D)),
            _const_spec((D, D_FF)), _const_spec((D_FF, D)),
            _const_spec((1, D)), _const_spec((1, D)),
        ],
        out_specs=row_spec,
        compiler_params=pltpu.CompilerParams(
            dimension_semantics=("parallel", "parallel", "parallel"),
            vmem_limit_bytes=VMEM_LIMIT),
        name="ffn_block",
    )(o, xv, wo, vec(g1), vec(b1), w1, w2, vec(g2), vec(b2))
    return out.reshape(B, S, D)


def _rotary_tables(S, n_res):
    half = ROPE_DIM // 2
    inv_freq = ROPE_THETA ** (-jnp.arange(half, dtype=F32) / half)
    pos = jnp.arange(S, dtype=jnp.int32).astype(F32)
    ang = pos[:, None] * inv_freq[None, :]
    cos, sin = jnp.cos(ang), jnp.sin(ang)
    ones = jnp.ones((S, HEAD_DIM - ROPE_DIM), F32)
    zeros8 = jnp.zeros((S, half), F32)
    zeros = jnp.zeros((S, HEAD_DIM - ROPE_DIM), F32)
    c = jnp.concatenate([cos, cos, ones], axis=1)
    sa = jnp.concatenate([-sin, zeros8, zeros], axis=1)
    sb = jnp.concatenate([zeros8, sin, zeros], axis=1)
    L = S // n_res

    def lay(t):
        t = jnp.concatenate([t, t], axis=1)
        return t.reshape(L, n_res, LANES).transpose(1, 0, 2)
    return lay(c), lay(sa), lay(sb)


def kernel(x, w_qkv_0, w_o_0, ln1_g_0, ln1_b_0, w_ff1_0, w_ff2_0, ln2_g_0, ln2_b_0,
           w_qkv_1, w_o_1, ln1_g_1, ln1_b_1, w_ff1_1, w_ff2_1, ln2_g_1, ln2_b_1):
    B, S, D = x.shape
    bf = lambda w: w.astype(BF16)

    qkv0 = _qkv_proj(x, bf(w_qkv_0), 1, LOG2E / math.sqrt(HEAD_DIM))
    o0 = _stick_breaking(qkv0)
    x = _ffn_block(o0, x, bf(w_o_0), ln1_g_0, ln1_b_0, bf(w_ff1_0), bf(w_ff2_0), ln2_g_0, ln2_b_0)

    qkv1 = _qkv_proj(x, bf(w_qkv_1), N_RES, LOG2E / math.sqrt(HEAD_DIM), tables=_rotary_tables(S, N_RES))
    o1 = _dilated(qkv1)
    x = _ffn_block(o1, x, bf(w_o_1), ln1_g_1, ln1_b_1, bf(w_ff1_1), bf(w_ff2_1), ln2_g_1, ln2_b_1)
    return x
```

```python
import functools
import math

import jax
import jax.numpy as jnp
from jax import lax
from jax.experimental import pallas as pl
from jax.experimental.pallas import tpu as pltpu

D_MODEL = 1024
N_HEADS = 16
HEAD_DIM = 64
D_FF = 4096
ROPE_THETA = 500000.0
ROPE_DIM = 16
DEPTH = 2
ALPHA = (2 * DEPTH) ** 0.25
LN_EPS = 1e-5
WINDOW_SUB = 128
N_RES = 16

LANES = 128
N_PAIRS = D_MODEL // LANES
N_COLS = 3 * N_PAIRS
VMEM_LIMIT = 56 * 1024 * 1024

LOG2E = 1.4426950408889634
SB_DONE = 160.0
SB_TQ = 128
SB_HK = 160
SB_PAD = 256

F32 = jnp.float32
BF16 = jnp.bfloat16


def _const_spec(shape):
    nd = len(shape)
    return pl.BlockSpec(shape, lambda *_: (0,) * nd, pipeline_mode=pl.Buffered(1))


def _qkv_kernel(*refs, rotary, q_scale):
    if rotary:
        x_ref, w_ref, c_ref, sa_ref, sb_ref, o_ref = refs
    else:
        x_ref, w_ref, o_ref = refs
    x = x_ref[0].astype(BF16)
    for part in range(3):
        y = jnp.dot(x, w_ref[:, part * D_MODEL:(part + 1) * D_MODEL],
                    preferred_element_type=F32)
        for j in range(N_PAIRS):
            col = y[:, j * LANES:(j + 1) * LANES]
            if rotary and part < 2:
                col = (col * c_ref[0, 0]
                       + pltpu.roll(col, LANES - ROPE_DIM // 2, 1) * sa_ref[0, 0]
                       + pltpu.roll(col, ROPE_DIM // 2, 1) * sb_ref[0, 0])
            if part == 0:
                col = col * q_scale
            o_ref[0, part * N_PAIRS + j, 0] = col.astype(BF16)


def _qkv_proj(x, w_bf16, n_res, q_scale, tables=None, tm=512):
    B, S, D = x.shape
    L = S // n_res
    xv = x.reshape(B, L, n_res * D)
    rotary = tables is not None
    in_specs = [
        pl.BlockSpec((1, tm, D), lambda b, r, i: (b, i, r)),
        _const_spec((D, 3 * D)),
    ]
    args = [xv, w_bf16]
    if rotary:
        for t in tables:
            in_specs.append(pl.BlockSpec((1, 1, tm, LANES), lambda b, r, i: (0, r, i, 0)))
            args.append(t.reshape(1, n_res, L, LANES))
    return pl.pallas_call(
        functools.partial(_qkv_kernel, rotary=rotary, q_scale=q_scale),
        out_shape=jax.ShapeDtypeStruct((B, N_COLS, n_res, L, LANES), BF16),
        grid=(B, n_res, L // tm),
        in_specs=in_specs,
        out_specs=pl.BlockSpec((1, N_COLS, 1, tm, LANES), lambda b, r, i: (b, 0, r, i, 0)),
        compiler_params=pltpu.CompilerParams(
            dimension_semantics=("parallel", "parallel", "parallel"),
            vmem_limit_bytes=VMEM_LIMIT),
        name="qkv_rot" if rotary else "qkv",
    )(*args)


def _qkv_perm_kernel(x_ref, w_ref, c_ref, sa_ref, sb_ref, o_ref, xs_ref, *, q_scale):
    tm = x_ref.shape[1]
    tl = tm // N_RES
    for j in range(N_PAIRS):
        xs_ref[j] = x_ref[0, :, j * LANES:(j + 1) * LANES]
    xp = jnp.concatenate(
        [jnp.concatenate([xs_ref[j, pl.ds(r, tl, stride=N_RES), :] for j in range(N_PAIRS)], axis=1)
         for r in range(N_RES)], axis=0).astype(BF16)
    cos = c_ref[...].reshape(tm, LANES)
    sa = sa_ref[...].reshape(tm, LANES)
    sb = sb_ref[...].reshape(tm, LANES)
    for part in range(3):
        y = jnp.dot(xp, w_ref[:, part * D_MODEL:(part + 1) * D_MODEL],
                    preferred_element_type=F32)
        for j in range(N_PAIRS):
            col = y[:, j * LANES:(j + 1) * LANES]
            if part < 2:
                col = (col * cos
                       + pltpu.roll(col, LANES - ROPE_DIM // 2, 1) * sa
                       + pltpu.roll(col, ROPE_DIM // 2, 1) * sb)
            if part == 0:
                col = col * q_scale
            o_ref[0, part * N_PAIRS + j] = col.astype(BF16).reshape(N_RES, tl, LANES)


def _qkv_proj_perm(x, w_bf16, q_scale, tables, tm=512):
    B, S, D = x.shape
    L = S // N_RES
    tl = tm // N_RES
    tab_spec = pl.BlockSpec((N_RES, tl, LANES), lambda b, i: (0, i, 0))
    return pl.pallas_call(
        functools.partial(_qkv_perm_kernel, q_scale=q_scale),
        out_shape=jax.ShapeDtypeStruct((B, N_COLS, N_RES, L, LANES), BF16),
        grid=(B, S // tm),
        in_specs=[pl.BlockSpec((1, tm, D), lambda b, i: (b, i, 0)), _const_spec((D, 3 * D)),
                  tab_spec, tab_spec, tab_spec],
        out_specs=pl.BlockSpec((1, N_COLS, N_RES, tl, LANES), lambda b, i: (b, 0, 0, i, 0)),
        scratch_shapes=[pltpu.VMEM((N_PAIRS, tm, LANES), F32)],
        compiler_params=pltpu.CompilerParams(
            dimension_semantics=("parallel", "parallel"),
            vmem_limit_bytes=VMEM_LIMIT),
        name="qkv_rot",
    )(x, w_bf16, *tables)


def _softplus2(z):
    neg_abs = lax.bitcast_convert_type(
        lax.bitcast_convert_type(z, jnp.uint32) | jnp.uint32(0x80000000), F32)
    return jnp.maximum(z, 0.0) + jnp.log2(1.0 + jnp.exp2(neg_abs))


def _sb_kernel(q_ref, k_ref, v_ref, o_ref, kp_ref, vp_ref, acc_ref, c_ref, *, group):
    S = q_ref.shape[3]
    tq, hk, pad = SB_TQ, SB_HK, SB_PAD
    nblk = S // tq

    kp_ref[pl.ds(0, pad), :] = jnp.zeros((pad, LANES), BF16)
    vp_ref[pl.ds(0, pad), :] = jnp.zeros((pad, LANES), BF16)

    def copy_in(i, _):
        r0 = pl.multiple_of(i * 512, 512)
        kp_ref[pl.ds(pad + r0, 512), :] = k_ref[0, 0, 0, pl.ds(r0, 512), :]
        vp_ref[pl.ds(pad + r0, 512), :] = v_ref[0, 0, 0, pl.ds(r0, 512), :]
        return 0
    lax.fori_loop(0, S // 512, copy_in, 0)

    lane = lax.broadcasted_iota(jnp.int32, (1, LANES), 1)
    head0 = lane < HEAD_DIM
    key_i = lax.broadcasted_iota(jnp.int32, (tq, 2 * tq), 0)
    qry_i = lax.broadcasted_iota(jnp.int32, (tq, 2 * tq), 1) & (tq - 1)
    causal = key_i < qry_i

    def suffix_ones(n):
        return (lax.broadcasted_iota(jnp.int32, (n, n), 1)
                >= lax.broadcasted_iota(jnp.int32, (n, n), 0)).astype(BF16)
    t_diag, t_hist = suffix_ones(tq), suffix_ones(hk)
    dim_head0 = lax.broadcasted_iota(jnp.int32, (LANES, tq), 0) < HEAD_DIM

    def q_stack(q0):
        q = q_ref[0, 0, 0, pl.ds(q0, tq), :]
        zero = jnp.zeros_like(q)
        return jnp.concatenate([jnp.where(head0, q, zero), jnp.where(head0, zero, q)], axis=0)

    def scores(qs, k0, n):
        kc = kp_ref[pl.ds(k0, n), :]
        return lax.dot_general(kc, qs, (((1,), (1,)), ((), ())), preferred_element_type=F32)

    def weighted_values(k0, n, w):
        vc = vp_ref[pl.ds(k0, n), :]
        return lax.dot_general(vc, w.astype(BF16), (((0,), (0,)), ((), ())),
                               preferred_element_type=F32)

    def chunk(qs, k0, n, t_mat, carry, mask):
        z = scores(qs, k0, n)
        sp = _softplus2(z)
        if mask is not None:
            sp = jnp.where(mask, sp, 0.0)
        r = jnp.dot(t_mat, sp.astype(BF16), preferred_element_type=F32)
        logw = z - r if carry is None else z - r - carry
        w = jnp.exp2(logw)
        if mask is not None:
            w = jnp.where(mask, w, 0.0)
        return weighted_values(k0, n, w), r[0:1, :]

    def fast(blks):
        n = len(blks)
        kd = [pl.multiple_of(pad + b * tq, tq) for b in blks]
        kh = [pl.multiple_of(pad + b * tq - hk, 32) for b in blks]
        qs = [q_stack(pl.multiple_of(b * tq, tq)) for b in blks]
        zd = [scores(qs[g], kd[g], tq) for g in range(n)]
        zh = [scores(qs[g], kh[g], hk) for g in range(n)]
        spd = [jnp.where(causal, _softplus2(z), 0.0) for z in zd]
        sph = [_softplus2(z) for z in zh]
        rd = [jnp.dot(t_diag, s.astype(BF16), preferred_element_type=F32) for s in spd]
        rh = [jnp.dot(t_hist, s.astype(BF16), preferred_element_type=F32) for s in sph]
        cd = [jnp.sum(s, axis=0, keepdims=True) for s in spd]
        wd = [jnp.where(causal, jnp.exp2(zd[g] - rd[g]), 0.0) for g in range(n)]
        wh = [jnp.exp2(zh[g] - rh[g] - cd[g]) for g in range(n)]
        cmins = []
        for g in range(n):
            acc_ref[g] = weighted_values(kd[g], tq, wd[g]) + weighted_values(kh[g], hk, wh[g])
            c = cd[g] + rh[g][0:1, :]
            c_ref[g] = c
            cmins.append(jnp.min(c))
        return cmins

    def earlier_keys(g, blk, cmin):
        q0 = pl.multiple_of(blk * tq, tq)

        def cond(carry):
            hi, cmin = carry
            return jnp.logical_and(hi > 0, cmin < SB_DONE)

        def body(carry):
            hi, _ = carry
            qs = q_stack(q0)
            k0 = pl.multiple_of(pad + hi - tq, 32)
            c = c_ref[g]
            acc, c_n = chunk(qs, k0, tq, t_diag, c, None)
            acc_ref[g] += acc
            c = c + c_n
            c_ref[g] = c
            return hi - tq, jnp.min(c)

        lax.while_loop(cond, body, (q0 - hk, cmin))

    def write_out(g, blk):
        q0 = pl.multiple_of(blk * tq, tq)
        acc = acc_ref[g]
        o_t = jnp.where(dim_head0, acc[:, :tq], acc[:, tq:])
        o_ref[0, 0, 0, pl.ds(q0, tq), :] = o_t.T.astype(BF16)

    def block_group(i, _):
        blks = [i * group + g for g in range(group)]
        cmins = fast(blks)
        for g in range(group):
            earlier_keys(g, blks[g], cmins[g])
        for g in range(group):
            write_out(g, blks[g])
        return 0

    lax.fori_loop(0, nblk // group, block_group, 0)


def _stick_breaking(qkv, group=8):
    B, _, _, S, _ = qkv.shape
    assert S % (SB_TQ * group) == 0 and S % 512 == 0
    blk = (1, 1, 1, S, LANES)
    return pl.pallas_call(
        functools.partial(_sb_kernel, group=group),
        out_shape=jax.ShapeDtypeStruct((B, N_PAIRS, 1, S, LANES), BF16),
        grid=(B, N_PAIRS),
        in_specs=[
            pl.BlockSpec(blk, lambda b, p: (b, p, 0, 0, 0)),
            pl.BlockSpec(blk, lambda b, p: (b, N_PAIRS + p, 0, 0, 0)),
            pl.BlockSpec(blk, lambda b, p: (b, 2 * N_PAIRS + p, 0, 0, 0)),
        ],
        out_specs=pl.BlockSpec(blk, lambda b, p: (b, p, 0, 0, 0)),
        scratch_shapes=[
            pltpu.VMEM((SB_PAD + S, LANES), BF16), pltpu.VMEM((SB_PAD + S, LANES), BF16),
            pltpu.VMEM((group, LANES, 2 * SB_TQ), F32), pltpu.VMEM((group, 1, 2 * SB_TQ), F32)],
        compiler_params=pltpu.CompilerParams(
            dimension_semantics=("parallel", "parallel"),
            vmem_limit_bytes=VMEM_LIMIT),
        name="stick_breaking",
    )(qkv, qkv, qkv)


_DIL_TILES = ((1, 128), (4, 32), (16, 16))


def _dil_bias(n_slab, q_rows, first):
    tq, tk = n_slab * q_rows, 2 * n_slab * q_rows
    kj = lax.broadcasted_iota(jnp.int32, (tk, 2 * tq), 0)
    qi = lax.broadcasted_iota(jnp.int32, (tk, 2 * tq), 1) & (tq - 1)
    sh = q_rows.bit_length() - 1
    a, i = qi >> sh, qi & (q_rows - 1)
    a2, j = kj >> (sh + 1), kj & (2 * q_rows - 1)
    dq = 0 if first else q_rows
    dist = n_slab * (dq + i - j) + (a - a2)
    ok = jnp.logical_and(dist >= 0, dist <= WINDOW_SUB)
    return jnp.where(ok, 0.0, -jnp.inf).astype(F32)


def _dil_kernel(q_ref, k_ref, v_ref, o_ref, b16_ref, b4_ref, b1_ref, lw_s, o_s, nat_s):
    L = q_ref.shape[3]
    lane = lax.broadcasted_iota(jnp.int32, (1, LANES), 1)
    head0 = lane < HEAD_DIM
    bias_refs = (b16_ref, b4_ref, b1_ref)
    for (n_slab, q_rows), b_ref in zip(_DIL_TILES, bias_refs):
        b_ref[0] = _dil_bias(n_slab, q_rows, False)
        b_ref[1] = _dil_bias(n_slab, q_rows, True)

    def q_stack(q):
        zero = jnp.zeros_like(q)
        return jnp.concatenate([jnp.where(head0, q, zero), jnp.where(head0, zero, q)], axis=0)

    def tiles(branch, specs, lag):
        n_slab, q_rows = _DIL_TILES[branch]
        tq = n_slab * q_rows
        dim_head0 = lax.broadcasted_iota(jnp.int32, (LANES, tq), 0) < HEAD_DIM

        def scores(spec):
            slabs, lq0, _ = spec
            lq0 = pl.multiple_of(lq0, q_rows)
            first = lq0 < q_rows
            kq0 = pl.multiple_of(jnp.maximum(lq0 - q_rows, 0), q_rows)
            q = jnp.concatenate([q_ref[0, 0, s, pl.ds(lq0, q_rows), :] for s in slabs], axis=0)
            k = jnp.concatenate([k_ref[0, 0, s, pl.ds(kq0, 2 * q_rows), :] for s in slabs], axis=0)
            bias = bias_refs[branch][first.astype(jnp.int32)]
            s = lax.dot_general(k, q_stack(q), (((1,), (1,)), ((), ())),
                                preferred_element_type=F32) + bias
            return s, kq0

        def finish(spec, s, kq0):
            slabs, _, out_off = spec
            v = jnp.concatenate([v_ref[0, 0, sl, pl.ds(kq0, 2 * q_rows), :] for sl in slabs], axis=0)
            m = jnp.max(s, axis=0, keepdims=True)
            p = jnp.exp2(s - m)
            l = jnp.sum(p, axis=0, keepdims=True)
            num = lax.dot_general(v, p.astype(BF16), (((0,), (0,)), ((), ())),
                                  preferred_element_type=F32)
            num = num * (1.0 / l)
            lw = jnp.broadcast_to(m + jnp.log2(l), (LANES, 2 * tq))
            o_t = jnp.where(dim_head0, num[:, :tq], num[:, tq:]).T
            lw_t = jnp.where(dim_head0, lw[:, :tq], lw[:, tq:]).T
            out_off = pl.multiple_of(out_off, q_rows)
            for n, sl in enumerate(slabs):
                rows = slice(n * q_rows, (n + 1) * q_rows)
                lw_s[branch, sl, pl.ds(out_off, q_rows), :] = lw_t[rows]
                o_s[branch, sl, pl.ds(out_off, q_rows), :] = o_t[rows]

        pending = []
        for t in range(len(specs) + lag):
            if t < len(specs):
                pending.append(scores(specs[t]))
            if t >= lag:
                finish(specs[t - lag], *pending[t - lag])

    def super_tile(jt, _):
        l0 = jt * WINDOW_SUB
        tiles(0, [([r], l0, 0) for r in range(N_RES)], 3)
        tiles(1, [([4 * a + c for a in range(4)], l0 + 32 * u, 32 * u)
                  for c in range(4) for u in range(4)], 3)
        tiles(2, [(list(range(N_RES)), l0 + 16 * u, 16 * u) for u in range(8)], 1)
        for r in range(N_RES):
            w0, w1, w2 = lw_s[0, r], lw_s[1, r], lw_s[2, r]
            wm = jnp.maximum(jnp.maximum(w0, w1), w2)
            e0, e1, e2 = jnp.exp2(w0 - wm), jnp.exp2(w1 - wm), jnp.exp2(w2 - wm)
            num = e0 * o_s[0, r] + e1 * o_s[1, r] + e2 * o_s[2, r]
            nat_s[pl.ds(r, WINDOW_SUB, stride=N_RES), :] = num / (e0 + e1 + e2)
        p0 = pl.multiple_of(l0 * N_RES, WINDOW_SUB * N_RES)
        o_ref[0, 0, 0, pl.ds(p0, WINDOW_SUB * N_RES), :] = nat_s[...].astype(BF16)
        return 0

    lax.fori_loop(0, L // WINDOW_SUB, super_tile, 0)


def _dilated(qkv):
    B, _, R, L, _ = qkv.shape
    blk = (1, 1, R, L, LANES)
    out_blk = (1, 1, 1, R * L, LANES)
    scr = pltpu.VMEM((3, R, WINDOW_SUB, LANES), F32)
    return pl.pallas_call(
        _dil_kernel,
        out_shape=jax.ShapeDtypeStruct((B, N_PAIRS, 1, R * L, LANES), BF16),
        grid=(B, N_PAIRS),
        in_specs=[
            pl.BlockSpec(blk, lambda b, p: (b, p, 0, 0, 0)),
            pl.BlockSpec(blk, lambda b, p: (b, N_PAIRS + p, 0, 0, 0)),
            pl.BlockSpec(blk, lambda b, p: (b, 2 * N_PAIRS + p, 0, 0, 0)),
        ],
        out_specs=pl.BlockSpec(out_blk, lambda b, p: (b, p, 0, 0, 0)),
        scratch_shapes=[
            pltpu.VMEM((2, 256, 256), F32), pltpu.VMEM((2, 256, 256), F32),
            pltpu.VMEM((2, 512, 512), F32), scr, scr,
            pltpu.VMEM((WINDOW_SUB * R, LANES), F32)],
        compiler_params=pltpu.CompilerParams(
            dimension_semantics=("parallel", "parallel"),
            vmem_limit_bytes=VMEM_LIMIT),
        name="dilated",
    )(qkv, qkv, qkv)


def _layer_norm(y, g, b):
    mu = jnp.mean(y, axis=-1, keepdims=True)
    d = y - mu
    var = jnp.mean(d * d, axis=-1, keepdims=True)
    return d * lax.rsqrt(var + LN_EPS) * g + b


def _ffn_kernel(o_ref, x_ref, wo_ref, g1_ref, b1_ref, w1_ref, w2_ref, g2_ref, b2_ref, out_ref):
    o = jnp.concatenate([o_ref[0, j, 0] for j in range(N_PAIRS)], axis=1)
    mix = jnp.dot(o, wo_ref[...], preferred_element_type=F32)
    x1 = _layer_norm(ALPHA * x_ref[0] + mix, g1_ref[...], b1_ref[...])
    x1b = x1.astype(BF16)
    f = None
    for c in range(D_FF // D_MODEL):
        cs = slice(c * D_MODEL, (c + 1) * D_MODEL)
        h = jnp.dot(x1b, w1_ref[:, cs], preferred_element_type=F32)
        h = jnp.square(jnp.maximum(h, 0.0)).astype(BF16)
        fc = jnp.dot(h, w2_ref[cs, :], preferred_element_type=F32)
        f = fc if f is None else f + fc
    out_ref[0] = _layer_norm(ALPHA * x1 + f, g2_ref[...], b2_ref[...])


def _ffn_block(o, x, wo, g1, b1, w1, w2, g2, b2, tm=512):
    B, S, D = x.shape
    n_res, L = o.shape[2], o.shape[3]
    xv = x.reshape(B, L, n_res * D)
    vec = lambda t: t.reshape(1, D)
    row_spec = pl.BlockSpec((1, tm, D), lambda b, r, i: (b, i, r))
    out = pl.pallas_call(
        _ffn_kernel,
        out_shape=jax.ShapeDtypeStruct((B, L, n_res * D), F32),
        grid=(B, n_res, L // tm),
        in_specs=[
            pl.BlockSpec((1, N_PAIRS, 1, tm, LANES), lambda b, r, i: (b, 0, r, i, 0)),
            row_spec,
            _const_spec((D, D)), _const_spec((1, D)), _const_spec((1, D)),
            _const_spec((D, D_FF)), _const_spec((D_FF, D)),
            _const_spec((1, D)), _const_spec((1, D)),
        ],
        out_specs=row_spec,
        compiler_params=pltpu.CompilerParams(
            dimension_semantics=("parallel", "parallel", "parallel"),
            vmem_limit_bytes=VMEM_LIMIT),
        name="ffn_block",
    )(o, xv, wo, vec(g1), vec(b1), w1, w2, vec(g2), vec(b2))
    return out.reshape(B, S, D)


def _rotary_tables(S, n_res):
    half = ROPE_DIM // 2
    inv_freq = ROPE_THETA ** (-jnp.arange(half, dtype=F32) / half)
    L = S // n_res
    pos = (n_res * jnp.arange(L, dtype=jnp.int32)[None, :]
           + jnp.arange(n_res, dtype=jnp.int32)[:, None]).reshape(S)
    ang = pos.astype(F32)[:, None] * inv_freq[None, :]
    cs = jnp.concatenate([jnp.cos(ang), jnp.sin(ang)], axis=1)
    d = jnp.arange(LANES, dtype=jnp.int32) % HEAD_DIM
    src = jnp.arange(2 * half, dtype=jnp.int32)[:, None]
    f = (d % half)[None, :]
    d = d[None, :]
    e_c = jnp.logical_and(src == f, d < ROPE_DIM).astype(F32)
    e_sa = -jnp.logical_and(src == half + f, d < half).astype(F32)
    e_sb = jnp.logical_and(src == half + f, jnp.logical_and(d >= half, d < ROPE_DIM)).astype(F32)
    expand = lambda e: jnp.dot(cs, e, precision=lax.Precision.HIGHEST)
    c = expand(e_c) + (d >= ROPE_DIM).astype(F32)
    return tuple(t.reshape(n_res, L, LANES) for t in (c, expand(e_sa), expand(e_sb)))


def kernel(x, w_qkv_0, w_o_0, ln1_g_0, ln1_b_0, w_ff1_0, w_ff2_0, ln2_g_0, ln2_b_0,
           w_qkv_1, w_o_1, ln1_g_1, ln1_b_1, w_ff1_1, w_ff2_1, ln2_g_1, ln2_b_1):
    B, S, D = x.shape
    bf = lambda w: w.astype(BF16)

    qkv0 = _qkv_proj(x, bf(w_qkv_0), 1, LOG2E / math.sqrt(HEAD_DIM))
    o0 = _stick_breaking(qkv0)
    x = _ffn_block(o0, x, bf(w_o_0), ln1_g_0, ln1_b_0, bf(w_ff1_0), bf(w_ff2_0), ln2_g_0, ln2_b_0)

    qkv1 = _qkv_proj_perm(x, bf(w_qkv_1), LOG2E / math.sqrt(HEAD_DIM), _rotary_tables(S, N_RES))
    o1 = _dilated(qkv1)
    x = _ffn_block(o1, x, bf(w_o_1), ln1_g_1, ln1_b_1, bf(w_ff1_1), bf(w_ff2_1), ln2_g_1, ln2_b_1)
    return x
```

```python
import functools
import math

import jax
import jax.numpy as jnp
from jax import lax
from jax.experimental import pallas as pl
from jax.experimental.pallas import tpu as pltpu

D_MODEL = 1024
N_HEADS = 16
HEAD_DIM = 64
D_FF = 4096
ROPE_THETA = 500000.0
ROPE_DIM = 16
DEPTH = 2
ALPHA = (2 * DEPTH) ** 0.25
LN_EPS = 1e-5
WINDOW_SUB = 128
N_RES = 16

LANES = 128
N_PAIRS = D_MODEL // LANES
N_COLS = 3 * N_PAIRS
VMEM_LIMIT = 56 * 1024 * 1024

LOG2E = 1.4426950408889634
SB_DONE = 152.0
SB_TQ = 128
SB_HK = 160
SB_PAD = 256

F32 = jnp.float32
BF16 = jnp.bfloat16
MASKED = -1e30


def _const_spec(shape):
    nd = len(shape)
    return pl.BlockSpec(shape, lambda *_: (0,) * nd, pipeline_mode=pl.Buffered(1))


def _qkv_kernel(x_ref, w_ref, o_ref, *, q_scale):
    x = x_ref[0].astype(BF16)
    for part in range(3):
        y = jnp.dot(x, w_ref[:, part * D_MODEL:(part + 1) * D_MODEL],
                    preferred_element_type=F32)
        for j in range(N_PAIRS):
            col = y[:, j * LANES:(j + 1) * LANES]
            if part == 0:
                col = col * q_scale
            o_ref[0, part * N_PAIRS + j, 0] = col.astype(BF16)


def _qkv_proj(x, w_bf16, q_scale, tm=512):
    B, S, D = x.shape
    return pl.pallas_call(
        functools.partial(_qkv_kernel, q_scale=q_scale),
        out_shape=jax.ShapeDtypeStruct((B, N_COLS, 1, S, LANES), BF16),
        grid=(B, S // tm),
        in_specs=[pl.BlockSpec((1, tm, D), lambda b, i: (b, i, 0)), _const_spec((D, 3 * D))],
        out_specs=pl.BlockSpec((1, N_COLS, 1, tm, LANES), lambda b, i: (b, 0, 0, i, 0)),
        compiler_params=pltpu.CompilerParams(
            dimension_semantics=("parallel", "parallel"),
            vmem_limit_bytes=VMEM_LIMIT),
        name="qkv",
    )(x, w_bf16)


def _qkv_perm_kernel(x_ref, w_ref, c_ref, sa_ref, sb_ref, o_ref, xs_ref, *, q_scale):
    tm = x_ref.shape[1]
    tl = tm // N_RES
    for j in range(N_PAIRS):
        xs_ref[j] = x_ref[0, :, j * LANES:(j + 1) * LANES]
    xp = jnp.concatenate(
        [jnp.concatenate([xs_ref[j, pl.ds(r, tl, stride=N_RES), :] for j in range(N_PAIRS)], axis=1)
         for r in range(N_RES)], axis=0).astype(BF16)
    cos = c_ref[...].reshape(tm, LANES)
    sa = sa_ref[...].reshape(tm, LANES)
    sb = sb_ref[...].reshape(tm, LANES)
    for part in range(3):
        y = jnp.dot(xp, w_ref[:, part * D_MODEL:(part + 1) * D_MODEL],
                    preferred_element_type=F32)
        for j in range(N_PAIRS):
            col = y[:, j * LANES:(j + 1) * LANES]
            if part < 2:
                col = (col * cos
                       + pltpu.roll(col, LANES - ROPE_DIM // 2, 1) * sa
                       + pltpu.roll(col, ROPE_DIM // 2, 1) * sb)
            if part == 0:
                col = col * q_scale
            o_ref[0, part * N_PAIRS + j] = col.astype(BF16).reshape(N_RES, tl, LANES)


def _qkv_proj_perm(x, w_bf16, q_scale, tables, tm=512):
    B, S, D = x.shape
    L = S // N_RES
    tl = tm // N_RES
    tab_spec = pl.BlockSpec((N_RES, tl, LANES), lambda b, i: (0, i, 0))
    return pl.pallas_call(
        functools.partial(_qkv_perm_kernel, q_scale=q_scale),
        out_shape=jax.ShapeDtypeStruct((B, N_COLS, N_RES, L, LANES), BF16),
        grid=(B, S // tm),
        in_specs=[pl.BlockSpec((1, tm, D), lambda b, i: (b, i, 0)), _const_spec((D, 3 * D)),
                  tab_spec, tab_spec, tab_spec],
        out_specs=pl.BlockSpec((1, N_COLS, N_RES, tl, LANES), lambda b, i: (b, 0, 0, i, 0)),
        scratch_shapes=[pltpu.VMEM((N_PAIRS, tm, LANES), F32)],
        compiler_params=pltpu.CompilerParams(
            dimension_semantics=("parallel", "parallel"),
            vmem_limit_bytes=VMEM_LIMIT),
        name="qkv_rot",
    )(x, w_bf16, *tables)


def _softplus2(z):
    neg_abs = lax.bitcast_convert_type(
        lax.bitcast_convert_type(z, jnp.uint32) | jnp.uint32(0x80000000), F32)
    return jnp.maximum(z, 0.0) + jnp.log2(1.0 + jnp.exp2(neg_abs))


def _sb_kernel(q_ref, k_ref, v_ref, o_ref, kp_ref, vp_ref, acc_ref, c_ref, *, group):
    S = q_ref.shape[3]
    tq, hk, pad = SB_TQ, SB_HK, SB_PAD
    nblk = S // tq

    kp_ref[pl.ds(0, pad), :] = jnp.zeros((pad, LANES), BF16)
    vp_ref[pl.ds(0, pad), :] = jnp.zeros((pad, LANES), BF16)

    def copy_in(i, _):
        r0 = pl.multiple_of(i * 512, 512)
        kp_ref[pl.ds(pad + r0, 512), :] = k_ref[0, 0, 0, pl.ds(r0, 512), :]
        vp_ref[pl.ds(pad + r0, 512), :] = v_ref[0, 0, 0, pl.ds(r0, 512), :]
        return 0
    lax.fori_loop(0, S // 512, copy_in, 0)

    lane = lax.broadcasted_iota(jnp.int32, (1, LANES), 1)
    head0 = lane < HEAD_DIM
    sq_r = lax.broadcasted_iota(jnp.int32, (tq, tq), 0)
    sq_c = lax.broadcasted_iota(jnp.int32, (tq, tq), 1)
    causal_bias = jnp.where(sq_r < sq_c, 0.0, MASKED).astype(BF16)
    one_hot = (sq_r == sq_c).astype(BF16)
    one_hot2 = jnp.concatenate([one_hot, one_hot], axis=0)

    def suffix_ones(n):
        return (lax.broadcasted_iota(jnp.int32, (n, n), 1)
                >= lax.broadcasted_iota(jnp.int32, (n, n), 0)).astype(BF16)
    t_diag, t_hist = suffix_ones(tq), suffix_ones(hk)
    dim_head0 = lax.broadcasted_iota(jnp.int32, (LANES, tq), 0) < HEAD_DIM

    def q_stack(q0):
        q = q_ref[0, 0, 0, pl.ds(q0, tq), :]
        zero = jnp.zeros_like(q)
        return jnp.concatenate([jnp.where(head0, q, zero), jnp.where(head0, zero, q)], axis=0)

    def scores(qs, k0, n, bias=None):
        kc = kp_ref[pl.ds(k0, n), :]
        if bias is not None:
            kc = jnp.concatenate([kc, bias], axis=1)
            qs = jnp.concatenate([qs, one_hot2], axis=1)
        return lax.dot_general(kc, qs, (((1,), (1,)), ((), ())), preferred_element_type=F32)

    def weighted_values(k0, n, w):
        vc = vp_ref[pl.ds(k0, n), :]
        return lax.dot_general(vc, w.astype(BF16), (((0,), (0,)), ((), ())),
                               preferred_element_type=F32)

    def chunk(qs, k0, n, t_mat, carry):
        z = scores(qs, k0, n)
        r = jnp.dot(t_mat, _softplus2(z).astype(BF16), preferred_element_type=F32)
        return weighted_values(k0, n, jnp.exp2(z - r - carry)), r[0:1, :]

    def fast(blks):
        n = len(blks)
        scored, summed, cmins = {}, {}, [None] * n
        for t in range(n + 2):
            if t < n:
                q0 = pl.multiple_of(blks[t] * tq, tq)
                kd = pl.multiple_of(pad + q0, tq)
                kh = pl.multiple_of(pad + q0 - hk, 32)
                qs = q_stack(q0)
                scored[t] = (scores(qs, kd, tq, causal_bias), scores(qs, kh, hk), kd, kh)
            if 0 <= t - 1 < n:
                zd, zh, kd, kh = scored.pop(t - 1)
                spd = _softplus2(zd)
                sph = _softplus2(zh)
                rd = jnp.dot(t_diag, spd.astype(BF16), preferred_element_type=F32)
                rh = jnp.dot(t_hist, sph.astype(BF16), preferred_element_type=F32)
                summed[t - 1] = (zd, zh, rd, rh, jnp.sum(spd, axis=0, keepdims=True), kd, kh)
            if 0 <= t - 2 < n:
                g = t - 2
                zd, zh, rd, rh, cd, kd, kh = summed.pop(g)
                wd = jnp.exp2(zd - rd)
                wh = jnp.exp2(zh - rh - cd)
                acc_ref[g] = weighted_values(kd, tq, wd) + weighted_values(kh, hk, wh)
                c = cd + rh[0:1, :]
                c_ref[g] = c
                cmins[g] = jnp.min(c)
        return cmins

    def earlier_keys(g, blk, cmin):
        q0 = pl.multiple_of(blk * tq, tq)

        def cond(carry):
            hi, cmin = carry
            return jnp.logical_and(hi > 0, cmin < SB_DONE)

        def body(carry):
            hi, _ = carry
            qs = q_stack(q0)
            k0 = pl.multiple_of(pad + hi - tq, 32)
            c = c_ref[g]
            acc, c_n = chunk(qs, k0, tq, t_diag, c)
            acc_ref[g] += acc
            c = c + c_n
            c_ref[g] = c
            return hi - tq, jnp.min(c)

        lax.while_loop(cond, body, (q0 - hk, cmin))

    def write_out(g, blk):
        q0 = pl.multiple_of(blk * tq, tq)
        acc = acc_ref[g]
        o_t = jnp.where(dim_head0, acc[:, :tq], acc[:, tq:])
        o_ref[0, 0, 0, pl.ds(q0, tq), :] = o_t.T.astype(BF16)

    def block_group(i, _):
        blks = [i * group + g for g in range(group)]
        cmins = fast(blks)
        for g in range(group):
            earlier_keys(g, blks[g], cmins[g])
        for g in range(group):
            write_out(g, blks[g])
        return 0

    lax.fori_loop(0, nblk // group, block_group, 0)


def _stick_breaking(qkv, group=16):
    B, _, _, S, _ = qkv.shape
    assert S % (SB_TQ * group) == 0 and S % 512 == 0
    blk = (1, 1, 1, S, LANES)
    return pl.pallas_call(
        functools.partial(_sb_kernel, group=group),
        out_shape=jax.ShapeDtypeStruct((B, N_PAIRS, 1, S, LANES), BF16),
        grid=(B, N_PAIRS),
        in_specs=[
            pl.BlockSpec(blk, lambda b, p: (b, p, 0, 0, 0)),
            pl.BlockSpec(blk, lambda b, p: (b, N_PAIRS + p, 0, 0, 0)),
            pl.BlockSpec(blk, lambda b, p: (b, 2 * N_PAIRS + p, 0, 0, 0)),
        ],
        out_specs=pl.BlockSpec(blk, lambda b, p: (b, p, 0, 0, 0)),
        scratch_shapes=[
            pltpu.VMEM((SB_PAD + S, LANES), BF16), pltpu.VMEM((SB_PAD + S, LANES), BF16),
            pltpu.VMEM((group, LANES, 2 * SB_TQ), F32), pltpu.VMEM((group, 1, 2 * SB_TQ), F32)],
        compiler_params=pltpu.CompilerParams(
            dimension_semantics=("parallel", "parallel"),
            vmem_limit_bytes=VMEM_LIMIT),
        name="stick_breaking",
    )(qkv, qkv, qkv)


_DIL_TILES = ((1, 1, 128), (4, 4, 32), (8, 16, 16))


def _dil_bias(branch, slab0, first):
    n_q, n_k, q_rows = _DIL_TILES[branch]
    tq, tk = n_q * q_rows, 2 * n_k * q_rows
    kj = lax.broadcasted_iota(jnp.int32, (tk, tq), 0)
    qi = lax.broadcasted_iota(jnp.int32, (tk, tq), 1)
    sh = q_rows.bit_length() - 1
    a, i = qi >> sh, qi & (q_rows - 1)
    a2, j = kj >> (sh + 1), kj & (2 * q_rows - 1)
    dq = 0 if first else q_rows
    dist = n_k * (dq + i - j) + (slab0 + a - a2)
    ok = jnp.logical_and(dist >= 0, dist <= WINDOW_SUB)
    return jnp.where(ok, 0.0, MASKED).astype(BF16)


def _dil_kernel(q_ref, k_ref, v_ref, o_ref, b16_ref, b4_ref, b1_ref, lw_s, o_s, nat_s):
    L = q_ref.shape[3]
    lane = lax.broadcasted_iota(jnp.int32, (1, LANES), 1)
    head0 = lane < HEAD_DIM
    bias_refs = (b16_ref, b4_ref, b1_ref)
    for branch, b_ref in enumerate(bias_refs):
        n_q, n_k, _ = _DIL_TILES[branch]
        for half in range(n_k // n_q):
            b_ref[half, 0] = _dil_bias(branch, half * n_q, False)
            b_ref[half, 1] = _dil_bias(branch, half * n_q, True)
    eye = (lax.broadcasted_iota(jnp.int32, (LANES, LANES), 0)
           == lax.broadcasted_iota(jnp.int32, (LANES, LANES), 1)).astype(BF16)
    one_hot2 = jnp.concatenate([eye, eye], axis=0)
    dim_head0 = lax.broadcasted_iota(jnp.int32, (LANES, LANES), 0) < HEAD_DIM

    def q_aug(q):
        zero = jnp.zeros_like(q)
        qs = jnp.concatenate([jnp.where(head0, q, zero), jnp.where(head0, zero, q)], axis=0)
        return jnp.concatenate([qs, one_hot2], axis=1)

    def tiles(branch, specs, lag):
        n_q, n_k, q_rows = _DIL_TILES[branch]
        stride = N_RES // n_k

        def scores(spec):
            slab0, lq0, _ = spec
            lq0 = pl.multiple_of(lq0, q_rows)
            first = (lq0 < q_rows).astype(jnp.int32)
            kq0 = pl.multiple_of(jnp.maximum(lq0 - q_rows, 0), q_rows)
            half = (slab0 // stride) // n_q if n_k > n_q else 0
            k_slabs = [(slab0 % stride) + stride * a for a in range(n_k)] if n_k == n_q else list(range(n_k))
            q_slabs = [slab0 + stride * a for a in range(n_q)]
            q = jnp.concatenate([q_ref[0, 0, sl, pl.ds(lq0, q_rows), :] for sl in q_slabs], axis=0)
            k = jnp.concatenate([k_ref[0, 0, sl, pl.ds(kq0, 2 * q_rows), :] for sl in k_slabs], axis=0)
            k = jnp.concatenate([k, bias_refs[branch][half, first]], axis=1)
            s = lax.dot_general(k, q_aug(q), (((1,), (1,)), ((), ())),
                                preferred_element_type=F32)
            return s, kq0, k_slabs, q_slabs

        def finish(spec, s, kq0, k_slabs, q_slabs):
            _, _, out_off = spec
            v = jnp.concatenate([v_ref[0, 0, sl, pl.ds(kq0, 2 * q_rows), :] for sl in k_slabs], axis=0)
            m = jnp.max(s, axis=0, keepdims=True)
            p = jnp.exp2(s - m)
            l = jnp.sum(p, axis=0, keepdims=True)
            num = lax.dot_general(v, p.astype(BF16), (((0,), (0,)), ((), ())),
                                  preferred_element_type=F32)
            num = num * (1.0 / l)
            lw = jnp.broadcast_to(m + jnp.log2(l), (LANES, 2 * LANES))
            o_t = jnp.where(dim_head0, num[:, :LANES], num[:, LANES:]).T
            lw_t = jnp.where(dim_head0, lw[:, :LANES], lw[:, LANES:]).T
            out_off = pl.multiple_of(out_off, q_rows)
            for n, sl in enumerate(q_slabs):
                rows = slice(n * q_rows, (n + 1) * q_rows)
                lw_s[branch, sl, pl.ds(out_off, q_rows), :] = lw_t[rows]
                o_s[branch, sl, pl.ds(out_off, q_rows), :] = o_t[rows]

        pending = []
        for t in range(len(specs) + lag):
            if t < len(specs):
                pending.append(scores(specs[t]))
            if t >= lag:
                finish(specs[t - lag], *pending[t - lag])

    def super_tile(jt, _):
        l0 = jt * WINDOW_SUB
        tiles(0, [(r, l0, 0) for r in range(N_RES)], 3)
        tiles(1, [(c, l0 + 32 * u, 32 * u) for c in range(4) for u in range(4)], 3)
        tiles(2, [(8 * half, l0 + 16 * u, 16 * u) for u in range(8) for half in range(2)], 3)
        for r in range(N_RES):
            w0, w1, w2 = lw_s[0, r], lw_s[1, r], lw_s[2, r]
            wm = jnp.maximum(jnp.maximum(w0, w1), w2)
            e0, e1, e2 = jnp.exp2(w0 - wm), jnp.exp2(w1 - wm), jnp.exp2(w2 - wm)
            num = e0 * o_s[0, r] + e1 * o_s[1, r] + e2 * o_s[2, r]
            nat_s[pl.ds(r, WINDOW_SUB, stride=N_RES), :] = num / (e0 + e1 + e2)
        p0 = pl.multiple_of(l0 * N_RES, WINDOW_SUB * N_RES)
        o_ref[0, 0, 0, pl.ds(p0, WINDOW_SUB * N_RES), :] = nat_s[...].astype(BF16)
        return 0

    lax.fori_loop(0, L // WINDOW_SUB, super_tile, 0)


def _dilated(qkv):
    B, _, R, L, _ = qkv.shape
    blk = (1, 1, R, L, LANES)
    out_blk = (1, 1, 1, R * L, LANES)
    scr = pltpu.VMEM((3, R, WINDOW_SUB, LANES), F32)
    return pl.pallas_call(
        _dil_kernel,
        out_shape=jax.ShapeDtypeStruct((B, N_PAIRS, 1, R * L, LANES), BF16),
        grid=(B, N_PAIRS),
        in_specs=[
            pl.BlockSpec(blk, lambda b, p: (b, p, 0, 0, 0)),
            pl.BlockSpec(blk, lambda b, p: (b, N_PAIRS + p, 0, 0, 0)),
            pl.BlockSpec(blk, lambda b, p: (b, 2 * N_PAIRS + p, 0, 0, 0)),
        ],
        out_specs=pl.BlockSpec(out_blk, lambda b, p: (b, p, 0, 0, 0)),
        scratch_shapes=[
            pltpu.VMEM((1, 2, 256, LANES), BF16), pltpu.VMEM((1, 2, 256, LANES), BF16),
            pltpu.VMEM((2, 2, 512, LANES), BF16), scr, scr,
            pltpu.VMEM((WINDOW_SUB * R, LANES), F32)],
        compiler_params=pltpu.CompilerParams(
            dimension_semantics=("parallel", "parallel"),
            vmem_limit_bytes=VMEM_LIMIT),
        name="dilated",
    )(qkv, qkv, qkv)


def _layer_norm(y, g, b):
    mu = jnp.mean(y, axis=-1, keepdims=True)
    d = y - mu
    var = jnp.mean(d * d, axis=-1, keepdims=True)
    return d * lax.rsqrt(var + LN_EPS) * g + b


def _ffn_kernel(o_ref, x_ref, wo_ref, g1_ref, b1_ref, w1_ref, w2_ref, g2_ref, b2_ref, out_ref, *, n_sub):
    tm = x_ref.shape[1]
    sm = tm // n_sub
    subs = range(n_sub)
    rows = [pl.ds(h * sm, sm) for h in subs]
    mix = [jnp.dot(jnp.concatenate([o_ref[0, j, 0, rows[h], :] for j in range(N_PAIRS)], axis=1),
                   wo_ref[...], preferred_element_type=F32) for h in subs]
    x1 = [_layer_norm(ALPHA * x_ref[0, rows[h], :] + mix[h], g1_ref[...], b1_ref[...]) for h in subs]
    x1b = [t.astype(BF16) for t in x1]
    f = [None] * n_sub
    for c in range(D_FF // D_MODEL):
        cs = slice(c * D_MODEL, (c + 1) * D_MODEL)
        hid = [jnp.dot(x1b[h], w1_ref[:, cs], preferred_element_type=F32) for h in subs]
        hid = [jnp.square(jnp.maximum(t, 0.0)).astype(BF16) for t in hid]
        for h in subs:
            fc = jnp.dot(hid[h], w2_ref[cs, :], preferred_element_type=F32)
            f[h] = fc if f[h] is None else f[h] + fc
    for h in subs:
        out_ref[0, rows[h], :] = _layer_norm(ALPHA * x1[h] + f[h], g2_ref[...], b2_ref[...])


def _ffn_block(o, x, wo, g1, b1, w1, w2, g2, b2, tm=1024, n_sub=4):
    B, S, D = x.shape
    vec = lambda t: t.reshape(1, D)
    row_spec = pl.BlockSpec((1, tm, D), lambda b, i: (b, i, 0))
    return pl.pallas_call(
        functools.partial(_ffn_kernel, n_sub=n_sub),
        out_shape=jax.ShapeDtypeStruct((B, S, D), F32),
        grid=(B, S // tm),
        in_specs=[
            pl.BlockSpec((1, N_PAIRS, 1, tm, LANES), lambda b, i: (b, 0, 0, i, 0)),
            row_spec,
            _const_spec((D, D)), _const_spec((1, D)), _const_spec((1, D)),
            _const_spec((D, D_FF)), _const_spec((D_FF, D)),
            _const_spec((1, D)), _const_spec((1, D)),
        ],
        out_specs=row_spec,
        compiler_params=pltpu.CompilerParams(
            dimension_semantics=("parallel", "parallel"),
            vmem_limit_bytes=VMEM_LIMIT),
        name="ffn_block",
    )(o, x, wo, vec(g1), vec(b1), w1, w2, vec(g2), vec(b2))


def _rotary_tables(S, n_res):
    half = ROPE_DIM // 2
    inv_freq = ROPE_THETA ** (-jnp.arange(half, dtype=F32) / half)
    L = S // n_res
    pos = (n_res * jnp.arange(L, dtype=jnp.int32)[None, :]
           + jnp.arange(n_res, dtype=jnp.int32)[:, None]).reshape(S)
    ang = pos.astype(F32)[:, None] * inv_freq[None, :]
    cs = jnp.concatenate([jnp.cos(ang), jnp.sin(ang)], axis=1)
    d = jnp.arange(LANES, dtype=jnp.int32) % HEAD_DIM
    src = jnp.arange(2 * half, dtype=jnp.int32)[:, None]
    f = (d % half)[None, :]
    d = d[None, :]
    e_c = jnp.logical_and(src == f, d < ROPE_DIM).astype(F32)
    e_sa = -jnp.logical_and(src == half + f, d < half).astype(F32)
    e_sb = jnp.logical_and(src == half + f, jnp.logical_and(d >= half, d < ROPE_DIM)).astype(F32)
    expand = lambda e: jnp.dot(cs, e, precision=lax.Precision.HIGHEST)
    c = expand(e_c) + (d >= ROPE_DIM).astype(F32)
    return tuple(t.reshape(n_res, L, LANES) for t in (c, expand(e_sa), expand(e_sb)))


def kernel(x, w_qkv_0, w_o_0, ln1_g_0, ln1_b_0, w_ff1_0, w_ff2_0, ln2_g_0, ln2_b_0,
           w_qkv_1, w_o_1, ln1_g_1, ln1_b_1, w_ff1_1, w_ff2_1, ln2_g_1, ln2_b_1):
    B, S, D = x.shape
    bf = lambda w: w.astype(BF16)

    qkv0 = _qkv_proj(x, bf(w_qkv_0), LOG2E / math.sqrt(HEAD_DIM))
    o0 = _stick_breaking(qkv0)
    x = _ffn_block(o0, x, bf(w_o_0), ln1_g_0, ln1_b_0, bf(w_ff1_0), bf(w_ff2_0), ln2_g_0, ln2_b_0)

    qkv1 = _qkv_proj_perm(x, bf(w_qkv_1), LOG2E / math.sqrt(HEAD_DIM), _rotary_tables(S, N_RES))
    o1 = _dilated(qkv1)
    x = _ffn_block(o1, x, bf(w_o_1), ln1_g_1, ln1_b_1, bf(w_ff1_1), bf(w_ff2_1), ln2_g_1, ln2_b_1)
    return x
```

```python
import functools
import math

import jax
import jax.numpy as jnp
from jax import lax
from jax.experimental import pallas as pl
from jax.experimental.pallas import tpu as pltpu

D_MODEL = 1024
N_HEADS = 16
HEAD_DIM = 64
D_FF = 4096
ROPE_THETA = 500000.0
ROPE_DIM = 16
DEPTH = 2
ALPHA = (2 * DEPTH) ** 0.25
LN_EPS = 1e-5
WINDOW_SUB = 128
N_RES = 16

LANES = 128
N_PAIRS = D_MODEL // LANES
N_COLS = 3 * N_PAIRS
VMEM_LIMIT = 56 * 1024 * 1024

LOG2E = 1.4426950408889634
SB_DONE = 152.0
SB_TQ = 128
SB_HK = 160
SB_PAD = 256

F32 = jnp.float32
BF16 = jnp.bfloat16
MASKED = -1e30


def _const_spec(shape):
    nd = len(shape)
    return pl.BlockSpec(shape, lambda *_: (0,) * nd, pipeline_mode=pl.Buffered(1))


def _qkv_kernel(x_ref, w_ref, o_ref, *, q_scale):
    x = x_ref[0].astype(BF16)
    for part in range(3):
        y = jnp.dot(x, w_ref[:, part * D_MODEL:(part + 1) * D_MODEL],
                    preferred_element_type=F32)
        for j in range(N_PAIRS):
            col = y[:, j * LANES:(j + 1) * LANES]
            if part == 0:
                col = col * q_scale
            o_ref[0, part * N_PAIRS + j, 0] = col.astype(BF16)


def _qkv_proj(x, w_bf16, q_scale, tm=512):
    B, S, D = x.shape
    return pl.pallas_call(
        functools.partial(_qkv_kernel, q_scale=q_scale),
        out_shape=jax.ShapeDtypeStruct((B, N_COLS, 1, S, LANES), BF16),
        grid=(B, S // tm),
        in_specs=[pl.BlockSpec((1, tm, D), lambda b, i: (b, i, 0)), _const_spec((D, 3 * D))],
        out_specs=pl.BlockSpec((1, N_COLS, 1, tm, LANES), lambda b, i: (b, 0, 0, i, 0)),
        compiler_params=pltpu.CompilerParams(
            dimension_semantics=("parallel", "parallel"),
            vmem_limit_bytes=VMEM_LIMIT),
        name="qkv",
    )(x, w_bf16)


def _qkv_perm_kernel(x_ref, w_ref, c_ref, sa_ref, sb_ref, o_ref, xs_ref, *, q_scale):
    tm = x_ref.shape[1]
    tl = tm // N_RES
    for j in range(N_PAIRS):
        xs_ref[j] = x_ref[0, :, j * LANES:(j + 1) * LANES]
    xp = jnp.concatenate(
        [jnp.concatenate([xs_ref[j, pl.ds(r, tl, stride=N_RES), :] for j in range(N_PAIRS)], axis=1)
         for r in range(N_RES)], axis=0).astype(BF16)
    cos = c_ref[...].reshape(tm, LANES)
    sa = sa_ref[...].reshape(tm, LANES)
    sb = sb_ref[...].reshape(tm, LANES)
    for part in range(3):
        y = jnp.dot(xp, w_ref[:, part * D_MODEL:(part + 1) * D_MODEL],
                    preferred_element_type=F32)
        for j in range(N_PAIRS):
            col = y[:, j * LANES:(j + 1) * LANES]
            if part < 2:
                col = (col * cos
                       + pltpu.roll(col, LANES - ROPE_DIM // 2, 1) * sa
                       + pltpu.roll(col, ROPE_DIM // 2, 1) * sb)
            if part == 0:
                col = col * q_scale
            o_ref[0, part * N_PAIRS + j] = col.astype(BF16).reshape(N_RES, tl, LANES)


def _qkv_proj_perm(x, w_bf16, q_scale, tables, tm=512):
    B, S, D = x.shape
    L = S // N_RES
    tl = tm // N_RES
    tab_spec = pl.BlockSpec((N_RES, tl, LANES), lambda b, i: (0, i, 0))
    return pl.pallas_call(
        functools.partial(_qkv_perm_kernel, q_scale=q_scale),
        out_shape=jax.ShapeDtypeStruct((B, N_COLS, N_RES, L, LANES), BF16),
        grid=(B, S // tm),
        in_specs=[pl.BlockSpec((1, tm, D), lambda b, i: (b, i, 0)), _const_spec((D, 3 * D)),
                  tab_spec, tab_spec, tab_spec],
        out_specs=pl.BlockSpec((1, N_COLS, N_RES, tl, LANES), lambda b, i: (b, 0, 0, i, 0)),
        scratch_shapes=[pltpu.VMEM((N_PAIRS, tm, LANES), F32)],
        compiler_params=pltpu.CompilerParams(
            dimension_semantics=("parallel", "parallel"),
            vmem_limit_bytes=VMEM_LIMIT),
        name="qkv_rot",
    )(x, w_bf16, *tables)


def _softplus2(z):
    neg_abs = lax.bitcast_convert_type(
        lax.bitcast_convert_type(z, jnp.uint32) | jnp.uint32(0x80000000), F32)
    return jnp.maximum(z, 0.0) + jnp.log2(1.0 + jnp.exp2(neg_abs))


def _sb_kernel(q_ref, k_ref, v_ref, o_ref, kp_ref, vp_ref, acc_ref, c_ref, *, group):
    S = q_ref.shape[3]
    tq, hk, pad = SB_TQ, SB_HK, SB_PAD
    nblk = S // tq

    kp_ref[pl.ds(0, pad), :] = jnp.zeros((pad, LANES), BF16)
    vp_ref[pl.ds(0, pad), :] = jnp.zeros((pad, LANES), BF16)

    def copy_in(i, _):
        r0 = pl.multiple_of(i * 512, 512)
        kp_ref[pl.ds(pad + r0, 512), :] = k_ref[0, 0, 0, pl.ds(r0, 512), :]
        vp_ref[pl.ds(pad + r0, 512), :] = v_ref[0, 0, 0, pl.ds(r0, 512), :]
        return 0
    lax.fori_loop(0, S // 512, copy_in, 0)

    lane = lax.broadcasted_iota(jnp.int32, (1, LANES), 1)
    head0 = lane < HEAD_DIM
    sq_r = lax.broadcasted_iota(jnp.int32, (tq, tq), 0)
    sq_c = lax.broadcasted_iota(jnp.int32, (tq, tq), 1)
    causal_bias = jnp.where(sq_r < sq_c, 0.0, MASKED).astype(BF16)
    one_hot = (sq_r == sq_c).astype(BF16)
    one_hot2 = jnp.concatenate([one_hot, one_hot], axis=1)

    def suffix_ones(n):
        return (lax.broadcasted_iota(jnp.int32, (n, n), 1)
                >= lax.broadcasted_iota(jnp.int32, (n, n), 0)).astype(BF16)
    t_diag, t_hist = suffix_ones(tq), suffix_ones(hk)
    dim_head0 = lax.broadcasted_iota(jnp.int32, (LANES, tq), 0) < HEAD_DIM

    def q_stack(q0):
        q = q_ref[0, 0, 0, pl.ds(q0, tq), :]
        zero = jnp.zeros_like(q)
        return jnp.concatenate([jnp.where(head0, q, zero), jnp.where(head0, zero, q)], axis=0).T

    def scores(qs, k0, n, bias=None):
        kc = kp_ref[pl.ds(k0, n), :]
        if bias is not None:
            kc = jnp.concatenate([kc, bias], axis=1)
            qs = jnp.concatenate([qs, one_hot2], axis=0)
        return jnp.dot(kc, qs, preferred_element_type=F32)

    def weighted_values(k0, n, w):
        vc = vp_ref[pl.ds(k0, n), :]
        return lax.dot_general(vc, w.astype(BF16), (((0,), (0,)), ((), ())),
                               preferred_element_type=F32)

    def chunk(qs, k0, n, t_mat, carry):
        z = scores(qs, k0, n)
        r = jnp.dot(t_mat, _softplus2(z).astype(BF16), preferred_element_type=F32)
        return weighted_values(k0, n, jnp.exp2(z - r - carry)), r[0:1, :]

    def fast(blks):
        n = len(blks)
        scored, summed, cmins = {}, {}, [None] * n
        for t in range(n + 2):
            if t < n:
                q0 = pl.multiple_of(blks[t] * tq, tq)
                kd = pl.multiple_of(pad + q0, tq)
                kh = pl.multiple_of(pad + q0 - hk, 32)
                qs = q_stack(q0)
                scored[t] = (scores(qs, kd, tq, causal_bias), scores(qs, kh, hk), kd, kh)
            if 0 <= t - 1 < n:
                zd, zh, kd, kh = scored.pop(t - 1)
                spd = _softplus2(zd)
                sph = _softplus2(zh)
                rd = jnp.dot(t_diag, spd.astype(BF16), preferred_element_type=F32)
                rh = jnp.dot(t_hist, sph.astype(BF16), preferred_element_type=F32)
                summed[t - 1] = (zd, zh, rd, rh, jnp.sum(spd, axis=0, keepdims=True), kd, kh)
            if 0 <= t - 2 < n:
                g = t - 2
                zd, zh, rd, rh, cd, kd, kh = summed.pop(g)
                wd = jnp.exp2(zd - rd)
                wh = jnp.exp2(zh - rh - cd)
                acc_ref[g] = weighted_values(kd, tq, wd) + weighted_values(kh, hk, wh)
                c = cd + rh[0:1, :]
                c_ref[g] = c
                cmins[g] = jnp.min(c)
        return cmins

    def earlier_keys(g, blk, cmin):
        q0 = pl.multiple_of(blk * tq, tq)

        def cond(carry):
            hi, cmin = carry
            return jnp.logical_and(hi > 0, cmin < SB_DONE)

        def body(carry):
            hi, _ = carry
            qs = q_stack(q0)
            k0 = pl.multiple_of(pad + hi - tq, 32)
            c = c_ref[g]
            acc, c_n = chunk(qs, k0, tq, t_diag, c)
            acc_ref[g] += acc
            c = c + c_n
            c_ref[g] = c
            return hi - tq, jnp.min(c)

        lax.while_loop(cond, body, (q0 - hk, cmin))

    def write_out(g, blk):
        q0 = pl.multiple_of(blk * tq, tq)
        acc = acc_ref[g]
        o_t = jnp.where(dim_head0, acc[:, :tq], acc[:, tq:])
        o_ref[0, 0, 0, pl.ds(q0, tq), :] = o_t.T.astype(BF16)

    def block_group(i, _):
        blks = [i * group + g for g in range(group)]
        cmins = fast(blks)
        for g in range(group):
            earlier_keys(g, blks[g], cmins[g])
        for g in range(group):
            write_out(g, blks[g])
        return 0

    lax.fori_loop(0, nblk // group, block_group, 0)


def _stick_breaking(qkv, group=16):
    B, _, _, S, _ = qkv.shape
    assert S % (SB_TQ * group) == 0 and S % 512 == 0
    blk = (1, 1, 1, S, LANES)
    return pl.pallas_call(
        functools.partial(_sb_kernel, group=group),
        out_shape=jax.ShapeDtypeStruct((B, N_PAIRS, 1, S, LANES), BF16),
        grid=(B, N_PAIRS),
        in_specs=[
            pl.BlockSpec(blk, lambda b, p: (b, p, 0, 0, 0)),
            pl.BlockSpec(blk, lambda b, p: (b, N_PAIRS + p, 0, 0, 0)),
            pl.BlockSpec(blk, lambda b, p: (b, 2 * N_PAIRS + p, 0, 0, 0)),
        ],
        out_specs=pl.BlockSpec(blk, lambda b, p: (b, p, 0, 0, 0)),
        scratch_shapes=[
            pltpu.VMEM((SB_PAD + S, LANES), BF16), pltpu.VMEM((SB_PAD + S, LANES), BF16),
            pltpu.VMEM((group, LANES, 2 * SB_TQ), F32), pltpu.VMEM((group, 1, 2 * SB_TQ), F32)],
        compiler_params=pltpu.CompilerParams(
            dimension_semantics=("parallel", "parallel"),
            vmem_limit_bytes=VMEM_LIMIT),
        name="stick_breaking",
    )(qkv, qkv, qkv)


_DIL_TILES = ((1, 1, 128), (4, 4, 32), (8, 16, 16))


def _dil_bias(branch, slab0, first):
    n_q, n_k, q_rows = _DIL_TILES[branch]
    tq, tk = n_q * q_rows, 2 * n_k * q_rows
    kj = lax.broadcasted_iota(jnp.int32, (tk, tq), 0)
    qi = lax.broadcasted_iota(jnp.int32, (tk, tq), 1)
    sh = q_rows.bit_length() - 1
    a, i = qi >> sh, qi & (q_rows - 1)
    a2, j = kj >> (sh + 1), kj & (2 * q_rows - 1)
    dq = 0 if first else q_rows
    dist = n_k * (dq + i - j) + (slab0 + a - a2)
    ok = jnp.logical_and(dist >= 0, dist <= WINDOW_SUB)
    return jnp.where(ok, 0.0, MASKED).astype(BF16)


def _dil_kernel(q_ref, k_ref, v_ref, o_ref, b16_ref, b4_ref, b1_ref, lw_s, o_s, nat_s):
    L = q_ref.shape[3]
    lane = lax.broadcasted_iota(jnp.int32, (1, LANES), 1)
    head0 = lane < HEAD_DIM
    bias_refs = (b16_ref, b4_ref, b1_ref)
    for branch, b_ref in enumerate(bias_refs):
        n_q, n_k, _ = _DIL_TILES[branch]
        for half in range(n_k // n_q):
            b_ref[half, 0] = _dil_bias(branch, half * n_q, False)
            b_ref[half, 1] = _dil_bias(branch, half * n_q, True)
    eye = (lax.broadcasted_iota(jnp.int32, (LANES, LANES), 0)
           == lax.broadcasted_iota(jnp.int32, (LANES, LANES), 1)).astype(BF16)
    one_hot2 = jnp.concatenate([eye, eye], axis=1)
    dim_head0 = lax.broadcasted_iota(jnp.int32, (LANES, LANES), 0) < HEAD_DIM

    def q_aug(q):
        zero = jnp.zeros_like(q)
        qs = jnp.concatenate([jnp.where(head0, q, zero), jnp.where(head0, zero, q)], axis=0)
        return jnp.concatenate([qs.T, one_hot2], axis=0)

    def tiles(branch, specs, lag):
        n_q, n_k, q_rows = _DIL_TILES[branch]
        stride = N_RES // n_k

        def scores(spec):
            slab0, lq0, _ = spec
            lq0 = pl.multiple_of(lq0, q_rows)
            first = (lq0 < q_rows).astype(jnp.int32)
            kq0 = pl.multiple_of(jnp.maximum(lq0 - q_rows, 0), q_rows)
            half = (slab0 // stride) // n_q if n_k > n_q else 0
            k_slabs = [(slab0 % stride) + stride * a for a in range(n_k)] if n_k == n_q else list(range(n_k))
            q_slabs = [slab0 + stride * a for a in range(n_q)]
            q = jnp.concatenate([q_ref[0, 0, sl, pl.ds(lq0, q_rows), :] for sl in q_slabs], axis=0)
            k = jnp.concatenate([k_ref[0, 0, sl, pl.ds(kq0, 2 * q_rows), :] for sl in k_slabs], axis=0)
            k = jnp.concatenate([k, bias_refs[branch][half, first]], axis=1)
            s = jnp.dot(k, q_aug(q), preferred_element_type=F32)
            return s, kq0, k_slabs, q_slabs

        def finish(spec, s, kq0, k_slabs, q_slabs):
            _, _, out_off = spec
            v = jnp.concatenate([v_ref[0, 0, sl, pl.ds(kq0, 2 * q_rows), :] for sl in k_slabs], axis=0)
            m = jnp.max(s, axis=0, keepdims=True)
            p = jnp.exp2(s - m)
            l = jnp.sum(p, axis=0, keepdims=True)
            num = lax.dot_general(v, p.astype(BF16), (((0,), (0,)), ((), ())),
                                  preferred_element_type=F32)
            num = num * (1.0 / l)
            lw = jnp.broadcast_to(m + jnp.log2(l), (LANES, 2 * LANES))
            o_t = jnp.where(dim_head0, num[:, :LANES], num[:, LANES:]).T
            lw_t = jnp.where(dim_head0, lw[:, :LANES], lw[:, LANES:]).T
            out_off = pl.multiple_of(out_off, q_rows)
            for n, sl in enumerate(q_slabs):
                rows = slice(n * q_rows, (n + 1) * q_rows)
                lw_s[branch, sl, pl.ds(out_off, q_rows), :] = lw_t[rows]
                o_s[branch, sl, pl.ds(out_off, q_rows), :] = o_t[rows]

        pending = []
        for t in range(len(specs) + lag):
            if t < len(specs):
                pending.append(scores(specs[t]))
            if t >= lag:
                finish(specs[t - lag], *pending[t - lag])

    def super_tile(jt, _):
        l0 = jt * WINDOW_SUB
        tiles(0, [(r, l0, 0) for r in range(N_RES)], 3)
        tiles(1, [(c, l0 + 32 * u, 32 * u) for c in range(4) for u in range(4)], 3)
        tiles(2, [(8 * half, l0 + 16 * u, 16 * u) for u in range(8) for half in range(2)], 3)
        for r in range(N_RES):
            w0, w1, w2 = lw_s[0, r], lw_s[1, r], lw_s[2, r]
            wm = jnp.maximum(jnp.maximum(w0, w1), w2)
            e0, e1, e2 = jnp.exp2(w0 - wm), jnp.exp2(w1 - wm), jnp.exp2(w2 - wm)
            num = e0 * o_s[0, r] + e1 * o_s[1, r] + e2 * o_s[2, r]
            nat_s[pl.ds(r, WINDOW_SUB, stride=N_RES), :] = num / (e0 + e1 + e2)
        p0 = pl.multiple_of(l0 * N_RES, WINDOW_SUB * N_RES)
        o_ref[0, 0, 0, pl.ds(p0, WINDOW_SUB * N_RES), :] = nat_s[...].astype(BF16)
        return 0

    lax.fori_loop(0, L // WINDOW_SUB, super_tile, 0)


def _dilated(qkv):
    B, _, R, L, _ = qkv.shape
    blk = (1, 1, R, L, LANES)
    out_blk = (1, 1, 1, R * L, LANES)
    scr = pltpu.VMEM((3, R, WINDOW_SUB, LANES), F32)
    return pl.pallas_call(
        _dil_kernel,
        out_shape=jax.ShapeDtypeStruct((B, N_PAIRS, 1, R * L, LANES), BF16),
        grid=(B, N_PAIRS),
        in_specs=[
            pl.BlockSpec(blk, lambda b, p: (b, p, 0, 0, 0)),
            pl.BlockSpec(blk, lambda b, p: (b, N_PAIRS + p, 0, 0, 0)),
            pl.BlockSpec(blk, lambda b, p: (b, 2 * N_PAIRS + p, 0, 0, 0)),
        ],
        out_specs=pl.BlockSpec(out_blk, lambda b, p: (b, p, 0, 0, 0)),
        scratch_shapes=[
            pltpu.VMEM((1, 2, 256, LANES), BF16), pltpu.VMEM((1, 2, 256, LANES), BF16),
            pltpu.VMEM((2, 2, 512, LANES), BF16), scr, scr,
            pltpu.VMEM((WINDOW_SUB * R, LANES), F32)],
        compiler_params=pltpu.CompilerParams(
            dimension_semantics=("parallel", "parallel"),
            vmem_limit_bytes=VMEM_LIMIT),
        name="dilated",
    )(qkv, qkv, qkv)


def _layer_norm(y, g, b):
    mu = jnp.mean(y, axis=-1, keepdims=True)
    d = y - mu
    var = jnp.mean(d * d, axis=-1, keepdims=True)
    return d * lax.rsqrt(var + LN_EPS) * g + b


def _ffn_kernel(o_ref, x_ref, wo_ref, g1_ref, b1_ref, w1_ref, w2_ref, g2_ref, b2_ref, out_ref, *, n_sub):
    tm = x_ref.shape[1]
    sm = tm // n_sub
    subs = range(n_sub)
    rows = [pl.ds(h * sm, sm) for h in subs]
    mix = [jnp.dot(jnp.concatenate([o_ref[0, j, 0, rows[h], :] for j in range(N_PAIRS)], axis=1),
                   wo_ref[...], preferred_element_type=F32) for h in subs]
    x1 = [_layer_norm(ALPHA * x_ref[0, rows[h], :] + mix[h], g1_ref[...], b1_ref[...]) for h in subs]
    x1b = [t.astype(BF16) for t in x1]
    f = [None] * n_sub
    for c in range(D_FF // D_MODEL):
        cs = slice(c * D_MODEL, (c + 1) * D_MODEL)
        hid = [jnp.dot(x1b[h], w1_ref[:, cs], preferred_element_type=F32) for h in subs]
        hid = [jnp.square(jnp.maximum(t, 0.0)).astype(BF16) for t in hid]
        for h in subs:
            fc = jnp.dot(hid[h], w2_ref[cs, :], preferred_element_type=F32)
            f[h] = fc if f[h] is None else f[h] + fc
    for h in subs:
        out_ref[0, rows[h], :] = _layer_norm(ALPHA * x1[h] + f[h], g2_ref[...], b2_ref[...])


def _ffn_block(o, x, wo, g1, b1, w1, w2, g2, b2, tm=1024, n_sub=4):
    B, S, D = x.shape
    vec = lambda t: t.reshape(1, D)
    row_spec = pl.BlockSpec((1, tm, D), lambda b, i: (b, i, 0))
    return pl.pallas_call(
        functools.partial(_ffn_kernel, n_sub=n_sub),
        out_shape=jax.ShapeDtypeStruct((B, S, D), F32),
        grid=(B, S // tm),
        in_specs=[
            pl.BlockSpec((1, N_PAIRS, 1, tm, LANES), lambda b, i: (b, 0, 0, i, 0)),
            row_spec,
            _const_spec((D, D)), _const_spec((1, D)), _const_spec((1, D)),
            _const_spec((D, D_FF)), _const_spec((D_FF, D)),
            _const_spec((1, D)), _const_spec((1, D)),
        ],
        out_specs=row_spec,
        compiler_params=pltpu.CompilerParams(
            dimension_semantics=("parallel", "parallel"),
            vmem_limit_bytes=VMEM_LIMIT),
        name="ffn_block",
    )(o, x, wo, vec(g1), vec(b1), w1, w2, vec(g2), vec(b2))


def _rotary_tables(S, n_res):
    half = ROPE_DIM // 2
    inv_freq = ROPE_THETA ** (-jnp.arange(half, dtype=F32) / half)
    L = S // n_res
    pos = (n_res * jnp.arange(L, dtype=jnp.int32)[None, :]
           + jnp.arange(n_res, dtype=jnp.int32)[:, None]).reshape(S)
    ang = pos.astype(F32)[:, None] * inv_freq[None, :]
    cs = jnp.concatenate([jnp.cos(ang), jnp.sin(ang)], axis=1)
    d = jnp.arange(LANES, dtype=jnp.int32) % HEAD_DIM
    src = jnp.arange(2 * half, dtype=jnp.int32)[:, None]
    f = (d % half)[None, :]
    d = d[None, :]
    e_c = jnp.logical_and(src == f, d < ROPE_DIM).astype(F32)
    e_sa = -jnp.logical_and(src == half + f, d < half).astype(F32)
    e_sb = jnp.logical_and(src == half + f, jnp.logical_and(d >= half, d < ROPE_DIM)).astype(F32)
    expand = lambda e: jnp.dot(cs, e, precision=lax.Precision.HIGHEST)
    c = expand(e_c) + (d >= ROPE_DIM).astype(F32)
    return tuple(t.reshape(n_res, L, LANES) for t in (c, expand(e_sa), expand(e_sb)))


def kernel(x, w_qkv_0, w_o_0, ln1_g_0, ln1_b_0, w_ff1_0, w_ff2_0, ln2_g_0, ln2_b_0,
           w_qkv_1, w_o_1, ln1_g_1, ln1_b_1, w_ff1_1, w_ff2_1, ln2_g_1, ln2_b_1):
    B, S, D = x.shape
    bf = lambda w: w.astype(BF16)

    qkv0 = _qkv_proj(x, bf(w_qkv_0), LOG2E / math.sqrt(HEAD_DIM))
    o0 = _stick_breaking(qkv0)
    x = _ffn_block(o0, x, bf(w_o_0), ln1_g_0, ln1_b_0, bf(w_ff1_0), bf(w_ff2_0), ln2_g_0, ln2_b_0)

    qkv1 = _qkv_proj_perm(x, bf(w_qkv_1), LOG2E / math.sqrt(HEAD_DIM), _rotary_tables(S, N_RES))
    o1 = _dilated(qkv1)
    x = _ffn_block(o1, x, bf(w_o_1), ln1_g_1, ln1_b_1, bf(w_ff1_1), bf(w_ff2_1), ln2_g_1, ln2_b_1)
    return x
```

```python
import functools
import math

import jax
import jax.numpy as jnp
from jax import lax
from jax.experimental import pallas as pl
from jax.experimental.pallas import tpu as pltpu

D_MODEL = 1024
N_HEADS = 16
HEAD_DIM = 64
D_FF = 4096
ROPE_THETA = 500000.0
ROPE_DIM = 16
DEPTH = 2
ALPHA = (2 * DEPTH) ** 0.25
LN_EPS = 1e-5
WINDOW_SUB = 128
N_RES = 16

LANES = 128
N_PAIRS = D_MODEL // LANES
N_COLS = 3 * N_PAIRS
VMEM_LIMIT = 56 * 1024 * 1024

LOG2E = 1.4426950408889634
SB_DONE = 152.0
SB_TQ = 128
SB_HK = 160
SB_PAD = 256

F32 = jnp.float32
BF16 = jnp.bfloat16
MASKED = -1e30


def _const_spec(shape):
    nd = len(shape)
    return pl.BlockSpec(shape, lambda *_: (0,) * nd, pipeline_mode=pl.Buffered(1))


def _qkv_kernel(x_ref, w_ref, o_ref, *, q_scale):
    x = x_ref[0].astype(BF16)
    for part in range(3):
        y = jnp.dot(x, w_ref[:, part * D_MODEL:(part + 1) * D_MODEL],
                    preferred_element_type=F32)
        for j in range(N_PAIRS):
            col = y[:, j * LANES:(j + 1) * LANES]
            if part == 0:
                col = col * q_scale
            o_ref[0, part * N_PAIRS + j, 0] = col.astype(BF16)


def _qkv_proj(x, w_bf16, q_scale, tm=512):
    B, S, D = x.shape
    return pl.pallas_call(
        functools.partial(_qkv_kernel, q_scale=q_scale),
        out_shape=jax.ShapeDtypeStruct((B, N_COLS, 1, S, LANES), BF16),
        grid=(B, S // tm),
        in_specs=[pl.BlockSpec((1, tm, D), lambda b, i: (b, i, 0)), _const_spec((D, 3 * D))],
        out_specs=pl.BlockSpec((1, N_COLS, 1, tm, LANES), lambda b, i: (b, 0, 0, i, 0)),
        compiler_params=pltpu.CompilerParams(
            dimension_semantics=("parallel", "parallel"),
            vmem_limit_bytes=VMEM_LIMIT),
        name="qkv",
    )(x, w_bf16)


def _qkv_perm_kernel(x_ref, w_ref, c_ref, sa_ref, sb_ref, o_ref, xs_ref, ys_ref, *, q_scale):
    tm = x_ref.shape[1]
    tl = tm // N_RES
    q4 = tm // 4
    for j in range(N_PAIRS):
        xs_ref[j] = x_ref[0, :, j * LANES:(j + 1) * LANES]
    for j in range(N_PAIRS):
        for c in range(4):
            ys_ref[j, pl.ds(c * q4, q4), :] = xs_ref[j, pl.ds(c, q4, stride=4), :]
    slabs = [(c, a) for c in range(4) for a in range(4)]
    xp = jnp.concatenate(
        [jnp.concatenate([ys_ref[j, pl.ds(c * q4 + a, tl, stride=4), :] for j in range(N_PAIRS)], axis=1)
         for c, a in slabs], axis=0).astype(BF16)
    regroup = lambda t: jnp.concatenate([t[4 * a + c] for c, a in slabs], axis=0)
    cos, sa, sb = regroup(c_ref), regroup(sa_ref), regroup(sb_ref)
    for part in range(3):
        y = jnp.dot(xp, w_ref[:, part * D_MODEL:(part + 1) * D_MODEL],
                    preferred_element_type=F32)
        for j in range(N_PAIRS):
            col = y[:, j * LANES:(j + 1) * LANES]
            if part < 2:
                col = (col * cos
                       + pltpu.roll(col, LANES - ROPE_DIM // 2, 1) * sa
                       + pltpu.roll(col, ROPE_DIM // 2, 1) * sb)
            if part == 0:
                col = col * q_scale
            col = col.astype(BF16)
            for n, (c, a) in enumerate(slabs):
                o_ref[0, part * N_PAIRS + j, 4 * a + c] = col[n * tl:(n + 1) * tl]


def _qkv_proj_perm(x, w_bf16, q_scale, tables, tm=512):
    B, S, D = x.shape
    L = S // N_RES
    tl = tm // N_RES
    tab_spec = pl.BlockSpec((N_RES, tl, LANES), lambda b, i: (0, i, 0))
    return pl.pallas_call(
        functools.partial(_qkv_perm_kernel, q_scale=q_scale),
        out_shape=jax.ShapeDtypeStruct((B, N_COLS, N_RES, L, LANES), BF16),
        grid=(B, S // tm),
        in_specs=[pl.BlockSpec((1, tm, D), lambda b, i: (b, i, 0)), _const_spec((D, 3 * D)),
                  tab_spec, tab_spec, tab_spec],
        out_specs=pl.BlockSpec((1, N_COLS, N_RES, tl, LANES), lambda b, i: (b, 0, 0, i, 0)),
        scratch_shapes=[pltpu.VMEM((N_PAIRS, tm, LANES), F32), pltpu.VMEM((N_PAIRS, tm, LANES), F32)],
        compiler_params=pltpu.CompilerParams(
            dimension_semantics=("parallel", "parallel"),
            vmem_limit_bytes=VMEM_LIMIT),
        name="qkv_rot",
    )(x, w_bf16, *tables)


def _softplus2(z):
    return jnp.where(z > 64.0, z, jnp.log2(1.0 + jnp.exp2(z)))


def _sb_kernel(q_ref, k_ref, v_ref, o_ref, kp_ref, vp_ref, acc_ref, c_ref, *, group):
    S = q_ref.shape[3]
    tq, hk, pad = SB_TQ, SB_HK, SB_PAD
    nblk = S // tq

    kp_ref[pl.ds(0, pad), :] = jnp.zeros((pad, LANES), BF16)
    vp_ref[pl.ds(0, pad), :] = jnp.zeros((pad, LANES), BF16)

    def copy_in(i, _):
        r0 = pl.multiple_of(i * 512, 512)
        kp_ref[pl.ds(pad + r0, 512), :] = k_ref[0, 0, 0, pl.ds(r0, 512), :]
        vp_ref[pl.ds(pad + r0, 512), :] = v_ref[0, 0, 0, pl.ds(r0, 512), :]
        return 0
    lax.fori_loop(0, S // 512, copy_in, 0)

    lane = lax.broadcasted_iota(jnp.int32, (1, LANES), 1)
    head0 = lane < HEAD_DIM
    sq_r = lax.broadcasted_iota(jnp.int32, (tq, tq), 0)
    sq_c = lax.broadcasted_iota(jnp.int32, (tq, tq), 1)
    causal_bias = jnp.where(sq_r < sq_c, 0.0, MASKED).astype(BF16)
    one_hot = (sq_r == sq_c).astype(BF16)
    one_hot2 = jnp.concatenate([one_hot, one_hot], axis=1)

    def suffix_ones(n):
        return (lax.broadcasted_iota(jnp.int32, (n, n), 1)
                >= lax.broadcasted_iota(jnp.int32, (n, n), 0)).astype(BF16)
    t_diag, t_hist = suffix_ones(tq), suffix_ones(hk)
    dim_head0 = lax.broadcasted_iota(jnp.int32, (LANES, tq), 0) < HEAD_DIM

    def q_stack(q0):
        q = q_ref[0, 0, 0, pl.ds(q0, tq), :]
        zero = jnp.zeros_like(q)
        return jnp.concatenate([jnp.where(head0, q, zero), jnp.where(head0, zero, q)], axis=0).T

    def scores(qs, k0, n, bias=None):
        kc = kp_ref[pl.ds(k0, n), :]
        if bias is not None:
            kc = jnp.concatenate([kc, bias], axis=1)
            qs = jnp.concatenate([qs, one_hot2], axis=0)
        return jnp.dot(kc, qs, preferred_element_type=F32)

    def weighted_values(k0, n, w):
        vc = vp_ref[pl.ds(k0, n), :]
        return lax.dot_general(vc, w.astype(BF16), (((0,), (0,)), ((), ())),
                               preferred_element_type=F32)

    def chunk(qs, k0, n, t_mat, carry):
        z = scores(qs, k0, n)
        r = jnp.dot(t_mat, _softplus2(z).astype(BF16), preferred_element_type=F32)
        return weighted_values(k0, n, jnp.exp2(z - r - carry)), r[0:1, :]

    def fast(blks):
        n = len(blks)
        scored, summed, cmins = {}, {}, [None] * n
        for t in range(n + 2):
            if t < n:
                q0 = pl.multiple_of(blks[t] * tq, tq)
                kd = pl.multiple_of(pad + q0, tq)
                kh = pl.multiple_of(pad + q0 - hk, 32)
                qs = q_stack(q0)
                scored[t] = (scores(qs, kd, tq, causal_bias), scores(qs, kh, hk), kd, kh)
            if 0 <= t - 1 < n:
                zd, zh, kd, kh = scored.pop(t - 1)
                spd = _softplus2(zd)
                sph = _softplus2(zh)
                rd = jnp.dot(t_diag, spd.astype(BF16), preferred_element_type=F32)
                rh = jnp.dot(t_hist, sph.astype(BF16), preferred_element_type=F32)
                summed[t - 1] = (zd, zh, rd, rh, rd[0:1, :], kd, kh)
            if 0 <= t - 2 < n:
                g = t - 2
                zd, zh, rd, rh, cd, kd, kh = summed.pop(g)
                wd = jnp.exp2(zd - rd)
                wh = jnp.exp2(zh - rh - cd)
                acc_ref[g] = weighted_values(kd, tq, wd) + weighted_values(kh, hk, wh)
                c = cd + rh[0:1, :]
                c_ref[g] = c
                cmins[g] = jnp.min(c)
        return cmins

    def earlier_keys(g, blk, cmin):
        q0 = pl.multiple_of(blk * tq, tq)

        def cond(carry):
            hi, cmin = carry
            return jnp.logical_and(hi > 0, cmin < SB_DONE)

        def body(carry):
            hi, _ = carry
            qs = q_stack(q0)
            k0 = pl.multiple_of(pad + hi - tq, 32)
            c = c_ref[g]
            acc, c_n = chunk(qs, k0, tq, t_diag, c)
            acc_ref[g] += acc
            c = c + c_n
            c_ref[g] = c
            return hi - tq, jnp.min(c)

        lax.while_loop(cond, body, (q0 - hk, cmin))

    def write_out(g, blk):
        q0 = pl.multiple_of(blk * tq, tq)
        acc = acc_ref[g]
        o_t = jnp.where(dim_head0, acc[:, :tq], acc[:, tq:])
        o_ref[0, 0, 0, pl.ds(q0, tq), :] = o_t.T.astype(BF16)

    def block_group(i, _):
        blks = [i * group + g for g in range(group)]
        cmins = fast(blks)
        for g in range(group):
            earlier_keys(g, blks[g], cmins[g])
        for g in range(group):
            write_out(g, blks[g])
        return 0

    lax.fori_loop(0, nblk // group, block_group, 0)


def _stick_breaking(qkv, group=16):
    B, _, _, S, _ = qkv.shape
    assert S % (SB_TQ * group) == 0 and S % 512 == 0
    blk = (1, 1, 1, S, LANES)
    return pl.pallas_call(
        functools.partial(_sb_kernel, group=group),
        out_shape=jax.ShapeDtypeStruct((B, N_PAIRS, 1, S, LANES), BF16),
        grid=(B, N_PAIRS),
        in_specs=[
            pl.BlockSpec(blk, lambda b, p: (b, p, 0, 0, 0)),
            pl.BlockSpec(blk, lambda b, p: (b, N_PAIRS + p, 0, 0, 0)),
            pl.BlockSpec(blk, lambda b, p: (b, 2 * N_PAIRS + p, 0, 0, 0)),
        ],
        out_specs=pl.BlockSpec(blk, lambda b, p: (b, p, 0, 0, 0)),
        scratch_shapes=[
            pltpu.VMEM((SB_PAD + S, LANES), BF16), pltpu.VMEM((SB_PAD + S, LANES), BF16),
            pltpu.VMEM((group, LANES, 2 * SB_TQ), F32), pltpu.VMEM((group, 1, 2 * SB_TQ), F32)],
        compiler_params=pltpu.CompilerParams(
            dimension_semantics=("parallel", "parallel"),
            vmem_limit_bytes=VMEM_LIMIT),
        name="stick_breaking",
    )(qkv, qkv, qkv)


_DIL_TILES = ((1, 1, 128), (4, 4, 32), (8, 16, 16))


def _dil_bias(branch, slab0, first):
    n_q, n_k, q_rows = _DIL_TILES[branch]
    tq, tk = n_q * q_rows, 2 * n_k * q_rows
    kj = lax.broadcasted_iota(jnp.int32, (tk, tq), 0)
    qi = lax.broadcasted_iota(jnp.int32, (tk, tq), 1)
    sh = q_rows.bit_length() - 1
    a, i = qi >> sh, qi & (q_rows - 1)
    a2, j = kj >> (sh + 1), kj & (2 * q_rows - 1)
    dq = 0 if first else q_rows
    dist = n_k * (dq + i - j) + (slab0 + a - a2)
    ok = jnp.logical_and(dist >= 0, dist <= WINDOW_SUB)
    return jnp.where(ok, 0.0, MASKED).astype(BF16)


def _dil_kernel(q_ref, k_ref, v_ref, o_ref, b16_ref, b4_ref, b1_ref, lw_s, o_s, nat_s):
    L = q_ref.shape[3]
    lane = lax.broadcasted_iota(jnp.int32, (1, LANES), 1)
    head0 = lane < HEAD_DIM
    bias_refs = (b16_ref, b4_ref, b1_ref)
    for branch, b_ref in enumerate(bias_refs):
        n_q, n_k, _ = _DIL_TILES[branch]
        for half in range(n_k // n_q):
            b_ref[half, 0] = _dil_bias(branch, half * n_q, False)
            b_ref[half, 1] = _dil_bias(branch, half * n_q, True)
    eye = (lax.broadcasted_iota(jnp.int32, (LANES, LANES), 0)
           == lax.broadcasted_iota(jnp.int32, (LANES, LANES), 1)).astype(BF16)
    one_hot2 = jnp.concatenate([eye, eye], axis=1)
    dim_head0 = lax.broadcasted_iota(jnp.int32, (LANES, LANES), 0) < HEAD_DIM

    def q_aug(q):
        zero = jnp.zeros_like(q)
        qs = jnp.concatenate([jnp.where(head0, q, zero), jnp.where(head0, zero, q)], axis=0)
        return jnp.concatenate([qs.T, one_hot2], axis=0)

    def tiles(branch, specs, lag):
        n_q, n_k, q_rows = _DIL_TILES[branch]
        stride = N_RES // n_k

        def scores(spec):
            slab0, lq0, _ = spec
            lq0 = pl.multiple_of(lq0, q_rows)
            first = (lq0 < q_rows).astype(jnp.int32)
            kq0 = pl.multiple_of(jnp.maximum(lq0 - q_rows, 0), q_rows)
            half = (slab0 // stride) // n_q if n_k > n_q else 0
            k_slabs = [(slab0 % stride) + stride * a for a in range(n_k)] if n_k == n_q else list(range(n_k))
            q_slabs = [slab0 + stride * a for a in range(n_q)]
            q = jnp.concatenate([q_ref[0, 0, sl, pl.ds(lq0, q_rows), :] for sl in q_slabs], axis=0)
            k = jnp.concatenate([k_ref[0, 0, sl, pl.ds(kq0, 2 * q_rows), :] for sl in k_slabs], axis=0)
            k = jnp.concatenate([k, bias_refs[branch][half, first]], axis=1)
            s = jnp.dot(k, q_aug(q), preferred_element_type=F32)
            return s, kq0, k_slabs, q_slabs

        def finish(spec, s, kq0, k_slabs, q_slabs):
            _, _, out_off = spec
            v = jnp.concatenate([v_ref[0, 0, sl, pl.ds(kq0, 2 * q_rows), :] for sl in k_slabs], axis=0)
            m = jnp.max(s, axis=0, keepdims=True)
            p = jnp.exp2(s - m)
            l = jnp.sum(p, axis=0, keepdims=True)
            num = lax.dot_general(v, p.astype(BF16), (((0,), (0,)), ((), ())),
                                  preferred_element_type=F32)
            num = num * (1.0 / l)
            lw = jnp.broadcast_to(m + jnp.log2(l), (LANES, 2 * LANES))
            o_t = jnp.where(dim_head0, num[:, :LANES], num[:, LANES:]).T
            lw_t = jnp.where(dim_head0, lw[:, :LANES], lw[:, LANES:]).T
            out_off = pl.multiple_of(out_off, q_rows)
            for n, sl in enumerate(q_slabs):
                rows = slice(n * q_rows, (n + 1) * q_rows)
                lw_s[branch, sl, pl.ds(out_off, q_rows), :] = lw_t[rows]
                o_s[branch, sl, pl.ds(out_off, q_rows), :] = o_t[rows]

        pending = []
        for t in range(len(specs) + lag):
            if t < len(specs):
                pending.append(scores(specs[t]))
            if t >= lag:
                finish(specs[t - lag], *pending[t - lag])

    def super_tile(jt, _):
        l0 = jt * WINDOW_SUB
        tiles(0, [(r, l0, 0) for r in range(N_RES)], 3)
        tiles(1, [(c, l0 + 32 * u, 32 * u) for c in range(4) for u in range(4)], 3)
        tiles(2, [(8 * half, l0 + 16 * u, 16 * u) for u in range(8) for half in range(2)], 3)
        for r in range(N_RES):
            w0, w1, w2 = lw_s[0, r], lw_s[1, r], lw_s[2, r]
            wm = jnp.maximum(jnp.maximum(w0, w1), w2)
            e0, e1, e2 = jnp.exp2(w0 - wm), jnp.exp2(w1 - wm), jnp.exp2(w2 - wm)
            num = e0 * o_s[0, r] + e1 * o_s[1, r] + e2 * o_s[2, r]
            nat_s[pl.ds(r, WINDOW_SUB, stride=N_RES), :] = num / (e0 + e1 + e2)
        p0 = pl.multiple_of(l0 * N_RES, WINDOW_SUB * N_RES)
        o_ref[0, 0, 0, pl.ds(p0, WINDOW_SUB * N_RES), :] = nat_s[...].astype(BF16)
        return 0

    lax.fori_loop(0, L // WINDOW_SUB, super_tile, 0)


def _dilated(qkv):
    B, _, R, L, _ = qkv.shape
    blk = (1, 1, R, L, LANES)
    out_blk = (1, 1, 1, R * L, LANES)
    scr = pltpu.VMEM((3, R, WINDOW_SUB, LANES), F32)
    return pl.pallas_call(
        _dil_kernel,
        out_shape=jax.ShapeDtypeStruct((B, N_PAIRS, 1, R * L, LANES), BF16),
        grid=(B, N_PAIRS),
        in_specs=[
            pl.BlockSpec(blk, lambda b, p: (b, p, 0, 0, 0)),
            pl.BlockSpec(blk, lambda b, p: (b, N_PAIRS + p, 0, 0, 0)),
            pl.BlockSpec(blk, lambda b, p: (b, 2 * N_PAIRS + p, 0, 0, 0)),
        ],
        out_specs=pl.BlockSpec(out_blk, lambda b, p: (b, p, 0, 0, 0)),
        scratch_shapes=[
            pltpu.VMEM((1, 2, 256, LANES), BF16), pltpu.VMEM((1, 2, 256, LANES), BF16),
            pltpu.VMEM((2, 2, 512, LANES), BF16), scr, scr,
            pltpu.VMEM((WINDOW_SUB * R, LANES), F32)],
        compiler_params=pltpu.CompilerParams(
            dimension_semantics=("parallel", "parallel"),
            vmem_limit_bytes=VMEM_LIMIT),
        name="dilated",
    )(qkv, qkv, qkv)


def _layer_norm(y, g, b):
    mu = jnp.mean(y, axis=-1, keepdims=True)
    d = y - mu
    var = jnp.mean(d * d, axis=-1, keepdims=True)
    return d * lax.rsqrt(var + LN_EPS) * g + b


def _ffn_kernel(o_ref, x_ref, wo_ref, g1_ref, b1_ref, w1_ref, w2_ref, g2_ref, b2_ref, out_ref, *, n_sub):
    tm = x_ref.shape[1]
    sm = tm // n_sub
    subs = range(n_sub)
    rows = [pl.ds(h * sm, sm) for h in subs]
    mix = [jnp.dot(jnp.concatenate([o_ref[0, j, 0, rows[h], :] for j in range(N_PAIRS)], axis=1),
                   wo_ref[...], preferred_element_type=F32) for h in subs]
    x1 = [_layer_norm(ALPHA * x_ref[0, rows[h], :] + mix[h], g1_ref[...], b1_ref[...]) for h in subs]
    x1b = [t.astype(BF16) for t in x1]
    f = [None] * n_sub
    for c in range(D_FF // D_MODEL):
        cs = slice(c * D_MODEL, (c + 1) * D_MODEL)
        hid = [jnp.dot(x1b[h], w1_ref[:, cs], preferred_element_type=F32) for h in subs]
        hid = [jnp.square(jnp.maximum(t, 0.0)).astype(BF16) for t in hid]
        for h in subs:
            fc = jnp.dot(hid[h], w2_ref[cs, :], preferred_element_type=F32)
            f[h] = fc if f[h] is None else f[h] + fc
    for h in subs:
        out_ref[0, rows[h], :] = _layer_norm(ALPHA * x1[h] + f[h], g2_ref[...], b2_ref[...])


def _ffn_block(o, x, wo, g1, b1, w1, w2, g2, b2, tm=1024, n_sub=4):
    B, S, D = x.shape
    vec = lambda t: t.reshape(1, D)
    row_spec = pl.BlockSpec((1, tm, D), lambda b, i: (b, i, 0))
    return pl.pallas_call(
        functools.partial(_ffn_kernel, n_sub=n_sub),
        out_shape=jax.ShapeDtypeStruct((B, S, D), F32),
        grid=(B, S // tm),
        in_specs=[
            pl.BlockSpec((1, N_PAIRS, 1, tm, LANES), lambda b, i: (b, 0, 0, i, 0)),
            row_spec,
            _const_spec((D, D)), _const_spec((1, D)), _const_spec((1, D)),
            _const_spec((D, D_FF)), _const_spec((D_FF, D)),
            _const_spec((1, D)), _const_spec((1, D)),
        ],
        out_specs=row_spec,
        compiler_params=pltpu.CompilerParams(
            dimension_semantics=("parallel", "parallel"),
            vmem_limit_bytes=VMEM_LIMIT),
        name="ffn_block",
    )(o, x, wo, vec(g1), vec(b1), w1, w2, vec(g2), vec(b2))


def _rotary_tables(S, n_res):
    half = ROPE_DIM // 2
    inv_freq = ROPE_THETA ** (-jnp.arange(half, dtype=F32) / half)
    L = S // n_res
    pos = (n_res * jnp.arange(L, dtype=jnp.int32)[None, :]
           + jnp.arange(n_res, dtype=jnp.int32)[:, None]).reshape(S)
    ang = pos.astype(F32)[:, None] * inv_freq[None, :]
    cs = jnp.concatenate([jnp.cos(ang), jnp.sin(ang)], axis=1)
    d = jnp.arange(LANES, dtype=jnp.int32) % HEAD_DIM
    src = jnp.arange(2 * half, dtype=jnp.int32)[:, None]
    f = (d % half)[None, :]
    d = d[None, :]
    e_c = jnp.logical_and(src == f, d < ROPE_DIM).astype(F32)
    e_sa = -jnp.logical_and(src == half + f, d < half).astype(F32)
    e_sb = jnp.logical_and(src == half + f, jnp.logical_and(d >= half, d < ROPE_DIM)).astype(F32)
    expand = lambda e: jnp.dot(cs, e, precision=lax.Precision.HIGHEST)
    c = expand(e_c) + (d >= ROPE_DIM).astype(F32)
    return tuple(t.reshape(n_res, L, LANES) for t in (c, expand(e_sa), expand(e_sb)))


def kernel(x, w_qkv_0, w_o_0, ln1_g_0, ln1_b_0, w_ff1_0, w_ff2_0, ln2_g_0, ln2_b_0,
           w_qkv_1, w_o_1, ln1_g_1, ln1_b_1, w_ff1_1, w_ff2_1, ln2_g_1, ln2_b_1):
    B, S, D = x.shape
    bf = lambda w: w.astype(BF16)

    qkv0 = _qkv_proj(x, bf(w_qkv_0), LOG2E / math.sqrt(HEAD_DIM))
    o0 = _stick_breaking(qkv0)
    x = _ffn_block(o0, x, bf(w_o_0), ln1_g_0, ln1_b_0, bf(w_ff1_0), bf(w_ff2_0), ln2_g_0, ln2_b_0)

    qkv1 = _qkv_proj_perm(x, bf(w_qkv_1), LOG2E / math.sqrt(HEAD_DIM), _rotary_tables(S, N_RES))
    o1 = _dilated(qkv1)
    x = _ffn_block(o1, x, bf(w_o_1), ln1_g_1, ln1_b_1, bf(w_ff1_1), bf(w_ff2_1), ln2_g_1, ln2_b_1)
    return x
```

```python
import functools
import math

import jax
import jax.numpy as jnp
from jax import lax
from jax.experimental import pallas as pl
from jax.experimental.pallas import tpu as pltpu

D_MODEL = 1024
N_HEADS = 16
HEAD_DIM = 64
D_FF = 4096
ROPE_THETA = 500000.0
ROPE_DIM = 16
DEPTH = 2
ALPHA = (2 * DEPTH) ** 0.25
LN_EPS = 1e-5
WINDOW_SUB = 128
N_RES = 16

LANES = 128
N_PAIRS = D_MODEL // LANES
N_COLS = 3 * N_PAIRS
VMEM_LIMIT = 56 * 1024 * 1024

LOG2E = 1.4426950408889634
SB_DONE = 152.0
SB_TQ = 128
SB_HK = 160
SB_PAD = 256

F32 = jnp.float32
BF16 = jnp.bfloat16
MASKED = -1e30


def _const_spec(shape):
    nd = len(shape)
    return pl.BlockSpec(shape, lambda *_: (0,) * nd, pipeline_mode=pl.Buffered(1))


def _qkv_kernel(x_ref, w_ref, o_ref, *, q_scale):
    x = x_ref[0].astype(BF16)
    for part in range(3):
        y = jnp.dot(x, w_ref[:, part * D_MODEL:(part + 1) * D_MODEL],
                    preferred_element_type=F32)
        for j in range(N_PAIRS):
            col = y[:, j * LANES:(j + 1) * LANES]
            if part == 0:
                col = col * q_scale
            o_ref[0, part * N_PAIRS + j, 0] = col.astype(BF16)


def _qkv_proj(x, w_bf16, q_scale, tm=512):
    B, S, D = x.shape
    return pl.pallas_call(
        functools.partial(_qkv_kernel, q_scale=q_scale),
        out_shape=jax.ShapeDtypeStruct((B, N_COLS, 1, S, LANES), BF16),
        grid=(B, S // tm),
        in_specs=[pl.BlockSpec((1, tm, D), lambda b, i: (b, i, 0)), _const_spec((D, 3 * D))],
        out_specs=pl.BlockSpec((1, N_COLS, 1, tm, LANES), lambda b, i: (b, 0, 0, i, 0)),
        compiler_params=pltpu.CompilerParams(
            dimension_semantics=("parallel", "parallel"),
            vmem_limit_bytes=VMEM_LIMIT),
        name="qkv",
    )(x, w_bf16)


def _qkv_perm_kernel(x_ref, w_ref, c_ref, sa_ref, sb_ref, o_ref, xs_ref, ys_ref, *, q_scale):
    tm = x_ref.shape[1]
    tl = tm // N_RES
    q4 = tm // 4
    for j in range(N_PAIRS):
        xs_ref[j] = x_ref[0, :, j * LANES:(j + 1) * LANES]
    for j in range(N_PAIRS):
        for c in range(4):
            ys_ref[j, pl.ds(c * q4, q4), :] = xs_ref[j, pl.ds(c, q4, stride=4), :]
    slabs = [(c, a) for c in range(4) for a in range(4)]
    xp = jnp.concatenate(
        [jnp.concatenate([ys_ref[j, pl.ds(c * q4 + a, tl, stride=4), :] for j in range(N_PAIRS)], axis=1)
         for c, a in slabs], axis=0).astype(BF16)
    regroup = lambda t: jnp.concatenate([t[4 * a + c] for c, a in slabs], axis=0)
    cos, sa, sb = regroup(c_ref), regroup(sa_ref), regroup(sb_ref)
    for part in range(3):
        y = jnp.dot(xp, w_ref[:, part * D_MODEL:(part + 1) * D_MODEL],
                    preferred_element_type=F32)
        for j in range(N_PAIRS):
            col = y[:, j * LANES:(j + 1) * LANES]
            if part < 2:
                col = (col * cos
                       + pltpu.roll(col, LANES - ROPE_DIM // 2, 1) * sa
                       + pltpu.roll(col, ROPE_DIM // 2, 1) * sb)
            if part == 0:
                col = col * q_scale
            col = col.astype(BF16)
            for n, (c, a) in enumerate(slabs):
                o_ref[0, part * N_PAIRS + j, 4 * a + c] = col[n * tl:(n + 1) * tl]


def _qkv_proj_perm(x, w_bf16, q_scale, tables, tm=512):
    B, S, D = x.shape
    L = S // N_RES
    tl = tm // N_RES
    tab_spec = pl.BlockSpec((N_RES, tl, LANES), lambda b, i: (0, i, 0))
    return pl.pallas_call(
        functools.partial(_qkv_perm_kernel, q_scale=q_scale),
        out_shape=jax.ShapeDtypeStruct((B, N_COLS, N_RES, L, LANES), BF16),
        grid=(B, S // tm),
        in_specs=[pl.BlockSpec((1, tm, D), lambda b, i: (b, i, 0)), _const_spec((D, 3 * D)),
                  tab_spec, tab_spec, tab_spec],
        out_specs=pl.BlockSpec((1, N_COLS, N_RES, tl, LANES), lambda b, i: (b, 0, 0, i, 0)),
        scratch_shapes=[pltpu.VMEM((N_PAIRS, tm, LANES), F32), pltpu.VMEM((N_PAIRS, tm, LANES), F32)],
        compiler_params=pltpu.CompilerParams(
            dimension_semantics=("parallel", "parallel"),
            vmem_limit_bytes=VMEM_LIMIT),
        name="qkv_rot",
    )(x, w_bf16, *tables)


def _softplus2(z):
    neg_abs = lax.bitcast_convert_type(
        lax.bitcast_convert_type(z, jnp.uint32) | jnp.uint32(0x80000000), F32)
    return jnp.maximum(z, 0.0) + jnp.log2(1.0 + jnp.exp2(neg_abs))


def _sb_kernel(q_ref, k_ref, v_ref, o_ref, kp_ref, vp_ref, acc_ref, c_ref, *, group):
    S = q_ref.shape[3]
    tq, hk, pad = SB_TQ, SB_HK, SB_PAD
    nblk = S // tq

    kp_ref[pl.ds(0, pad), :] = jnp.zeros((pad, LANES), BF16)
    vp_ref[pl.ds(0, pad), :] = jnp.zeros((pad, LANES), BF16)

    def copy_in(i, _):
        r0 = pl.multiple_of(i * 512, 512)
        kp_ref[pl.ds(pad + r0, 512), :] = k_ref[0, 0, 0, pl.ds(r0, 512), :]
        vp_ref[pl.ds(pad + r0, 512), :] = v_ref[0, 0, 0, pl.ds(r0, 512), :]
        return 0
    lax.fori_loop(0, S // 512, copy_in, 0)

    lane = lax.broadcasted_iota(jnp.int32, (1, LANES), 1)
    head0 = lane < HEAD_DIM
    sq_r = lax.broadcasted_iota(jnp.int32, (tq, tq), 0)
    sq_c = lax.broadcasted_iota(jnp.int32, (tq, tq), 1)
    causal_bias = jnp.where(sq_r < sq_c, 0.0, MASKED).astype(BF16)
    one_hot = (sq_r == sq_c).astype(BF16)
    one_hot2 = jnp.concatenate([one_hot, one_hot], axis=1)

    def suffix_ones(n):
        return (lax.broadcasted_iota(jnp.int32, (n, n), 1)
                >= lax.broadcasted_iota(jnp.int32, (n, n), 0)).astype(BF16)
    t_diag, t_hist = suffix_ones(tq), suffix_ones(hk)
    dim_head0 = lax.broadcasted_iota(jnp.int32, (LANES, tq), 0) < HEAD_DIM

    def q_stack(q0):
        q = q_ref[0, 0, 0, pl.ds(q0, tq), :]
        zero = jnp.zeros_like(q)
        return jnp.concatenate([jnp.where(head0, q, zero), jnp.where(head0, zero, q)], axis=0).T

    def scores(qs, k0, n, bias=None):
        kc = kp_ref[pl.ds(k0, n), :]
        if bias is not None:
            kc = jnp.concatenate([kc, bias], axis=1)
            qs = jnp.concatenate([qs, one_hot2], axis=0)
        return jnp.dot(kc, qs, preferred_element_type=F32)

    def weighted_values(k0, n, w):
        vc = vp_ref[pl.ds(k0, n), :]
        return lax.dot_general(vc, w.astype(BF16), (((0,), (0,)), ((), ())),
                               preferred_element_type=F32)

    def chunk(qs, k0, n, t_mat, carry):
        z = scores(qs, k0, n)
        r = jnp.dot(t_mat, _softplus2(z).astype(BF16), preferred_element_type=F32)
        return weighted_values(k0, n, jnp.exp2(z - r - carry)), r[0:1, :]

    def fast(blks):
        n = len(blks)
        scored, summed, cmins = {}, {}, [None] * n
        for t in range(n + 2):
            if t < n:
                q0 = pl.multiple_of(blks[t] * tq, tq)
                kd = pl.multiple_of(pad + q0, tq)
                kh = pl.multiple_of(pad + q0 - hk, 32)
                qs = q_stack(q0)
                scored[t] = (scores(qs, kd, tq, causal_bias), scores(qs, kh, hk), kd, kh)
            if 0 <= t - 1 < n:
                zd, zh, kd, kh = scored.pop(t - 1)
                spd = _softplus2(zd)
                sph = _softplus2(zh)
                rd = jnp.dot(t_diag, spd.astype(BF16), preferred_element_type=F32)
                rh = jnp.dot(t_hist, sph.astype(BF16), preferred_element_type=F32)
                summed[t - 1] = (zd, zh, rd, rh, jnp.sum(spd, axis=0, keepdims=True), kd, kh)
            if 0 <= t - 2 < n:
                g = t - 2
                zd, zh, rd, rh, cd, kd, kh = summed.pop(g)
                wd = jnp.exp2(zd - rd)
                wh = jnp.exp2(zh - rh - cd)
                acc_ref[g] = weighted_values(kd, tq, wd) + weighted_values(kh, hk, wh)
                c = cd + rh[0:1, :]
                c_ref[g] = c
                cmins[g] = jnp.min(c)
        return cmins

    def earlier_keys(g, blk, cmin):
        q0 = pl.multiple_of(blk * tq, tq)

        def cond(carry):
            hi, cmin = carry
            return jnp.logical_and(hi > 0, cmin < SB_DONE)

        def body(carry):
            hi, _ = carry
            qs = q_stack(q0)
            k0 = pl.multiple_of(pad + hi - tq, 32)
            c = c_ref[g]
            acc, c_n = chunk(qs, k0, tq, t_diag, c)
            acc_ref[g] += acc
            c = c + c_n
            c_ref[g] = c
            return hi - tq, jnp.min(c)

        lax.while_loop(cond, body, (q0 - hk, cmin))

    def write_out(g, blk):
        q0 = pl.multiple_of(blk * tq, tq)
        acc = acc_ref[g]
        o_t = jnp.where(dim_head0, acc[:, :tq], acc[:, tq:])
        o_ref[0, 0, 0, pl.ds(q0, tq), :] = o_t.T.astype(BF16)

    def block_group(i, _):
        blks = [i * group + g for g in range(group)]
        cmins = fast(blks)
        for g in range(group):
            earlier_keys(g, blks[g], cmins[g])
        for g in range(group):
            write_out(g, blks[g])
        return 0

    lax.fori_loop(0, nblk // group, block_group, 0)


def _stick_breaking(qkv, group=32):
    B, _, _, S, _ = qkv.shape
    assert S % (SB_TQ * group) == 0 and S % 512 == 0
    blk = (1, 1, 1, S, LANES)
    return pl.pallas_call(
        functools.partial(_sb_kernel, group=group),
        out_shape=jax.ShapeDtypeStruct((B, N_PAIRS, 1, S, LANES), BF16),
        grid=(B, N_PAIRS),
        in_specs=[
            pl.BlockSpec(blk, lambda b, p: (b, p, 0, 0, 0)),
            pl.BlockSpec(blk, lambda b, p: (b, N_PAIRS + p, 0, 0, 0)),
            pl.BlockSpec(blk, lambda b, p: (b, 2 * N_PAIRS + p, 0, 0, 0)),
        ],
        out_specs=pl.BlockSpec(blk, lambda b, p: (b, p, 0, 0, 0)),
        scratch_shapes=[
            pltpu.VMEM((SB_PAD + S, LANES), BF16), pltpu.VMEM((SB_PAD + S, LANES), BF16),
            pltpu.VMEM((group, LANES, 2 * SB_TQ), F32), pltpu.VMEM((group, 1, 2 * SB_TQ), F32)],
        compiler_params=pltpu.CompilerParams(
            dimension_semantics=("parallel", "parallel"),
            vmem_limit_bytes=VMEM_LIMIT),
        name="stick_breaking",
    )(qkv, qkv, qkv)


_DIL_TILES = ((1, 1, 128), (4, 4, 32), (8, 16, 16))


def _dil_bias(branch, slab0, first):
    n_q, n_k, q_rows = _DIL_TILES[branch]
    tq, tk = n_q * q_rows, 2 * n_k * q_rows
    kj = lax.broadcasted_iota(jnp.int32, (tk, tq), 0)
    qi = lax.broadcasted_iota(jnp.int32, (tk, tq), 1)
    sh = q_rows.bit_length() - 1
    a, i = qi >> sh, qi & (q_rows - 1)
    a2, j = kj >> (sh + 1), kj & (2 * q_rows - 1)
    dq = 0 if first else q_rows
    dist = n_k * (dq + i - j) + (slab0 + a - a2)
    ok = jnp.logical_and(dist >= 0, dist <= WINDOW_SUB)
    return jnp.where(ok, 0.0, MASKED).astype(BF16)


def _dil_kernel(q_ref, k_ref, v_ref, o_ref, b16_ref, b4_ref, b1_ref, lw_s, o_s, nat_s):
    L = q_ref.shape[3]
    lane = lax.broadcasted_iota(jnp.int32, (1, LANES), 1)
    head0 = lane < HEAD_DIM
    bias_refs = (b16_ref, b4_ref, b1_ref)
    for branch, b_ref in enumerate(bias_refs):
        n_q, n_k, _ = _DIL_TILES[branch]
        for half in range(n_k // n_q):
            b_ref[half, 0] = _dil_bias(branch, half * n_q, False)
            b_ref[half, 1] = _dil_bias(branch, half * n_q, True)
    eye = (lax.broadcasted_iota(jnp.int32, (LANES, LANES), 0)
           == lax.broadcasted_iota(jnp.int32, (LANES, LANES), 1)).astype(BF16)
    one_hot2 = jnp.concatenate([eye, eye], axis=1)
    dim_head0 = lax.broadcasted_iota(jnp.int32, (LANES, LANES), 0) < HEAD_DIM

    def q_aug(q):
        zero = jnp.zeros_like(q)
        qs = jnp.concatenate([jnp.where(head0, q, zero), jnp.where(head0, zero, q)], axis=0)
        return jnp.concatenate([qs.T, one_hot2], axis=0)

    def tiles(branch, specs, lag):
        n_q, n_k, q_rows = _DIL_TILES[branch]
        stride = N_RES // n_k

        def scores(spec):
            slab0, lq0, _ = spec
            lq0 = pl.multiple_of(lq0, q_rows)
            first = (lq0 < q_rows).astype(jnp.int32)
            kq0 = pl.multiple_of(jnp.maximum(lq0 - q_rows, 0), q_rows)
            half = (slab0 // stride) // n_q if n_k > n_q else 0
            k_slabs = [(slab0 % stride) + stride * a for a in range(n_k)] if n_k == n_q else list(range(n_k))
            q_slabs = [slab0 + stride * a for a in range(n_q)]
            q = jnp.concatenate([q_ref[0, 0, sl, pl.ds(lq0, q_rows), :] for sl in q_slabs], axis=0)
            k = jnp.concatenate([k_ref[0, 0, sl, pl.ds(kq0, 2 * q_rows), :] for sl in k_slabs], axis=0)
            k = jnp.concatenate([k, bias_refs[branch][half, first]], axis=1)
            s = jnp.dot(k, q_aug(q), preferred_element_type=F32)
            return s, kq0, k_slabs, q_slabs

        def finish(spec, s, kq0, k_slabs, q_slabs):
            _, _, out_off = spec
            v = jnp.concatenate([v_ref[0, 0, sl, pl.ds(kq0, 2 * q_rows), :] for sl in k_slabs], axis=0)
            m = jnp.max(s, axis=0, keepdims=True)
            p = jnp.exp2(s - m)
            l = jnp.sum(p, axis=0, keepdims=True)
            num = lax.dot_general(v, p.astype(BF16), (((0,), (0,)), ((), ())),
                                  preferred_element_type=F32)
            num = num * (1.0 / l)
            lw = jnp.broadcast_to(m + jnp.log2(l), (LANES, 2 * LANES))
            o_t = jnp.where(dim_head0, num[:, :LANES], num[:, LANES:]).T
            lw_t = jnp.where(dim_head0, lw[:, :LANES], lw[:, LANES:]).T
            out_off = pl.multiple_of(out_off, q_rows)
            for n, sl in enumerate(q_slabs):
                rows = slice(n * q_rows, (n + 1) * q_rows)
                lw_s[branch, sl, pl.ds(out_off, q_rows), :] = lw_t[rows]
                o_s[branch, sl, pl.ds(out_off, q_rows), :] = o_t[rows]

        pending = []
        for t in range(len(specs) + lag):
            if t < len(specs):
                pending.append(scores(specs[t]))
            if t >= lag:
                finish(specs[t - lag], *pending[t - lag])

    def super_tile(jt, _):
        l0 = jt * WINDOW_SUB
        tiles(0, [(r, l0, 0) for r in range(N_RES)], 3)
        tiles(1, [(c, l0 + 32 * u, 32 * u) for c in range(4) for u in range(4)], 3)
        tiles(2, [(8 * half, l0 + 16 * u, 16 * u) for u in range(8) for half in range(2)], 3)
        for r in range(N_RES):
            w0, w1, w2 = lw_s[0, r], lw_s[1, r], lw_s[2, r]
            wm = jnp.maximum(jnp.maximum(w0, w1), w2)
            e0, e1, e2 = jnp.exp2(w0 - wm), jnp.exp2(w1 - wm), jnp.exp2(w2 - wm)
            num = e0 * o_s[0, r] + e1 * o_s[1, r] + e2 * o_s[2, r]
            nat_s[pl.ds(r, WINDOW_SUB, stride=N_RES), :] = num / (e0 + e1 + e2)
        p0 = pl.multiple_of(l0 * N_RES, WINDOW_SUB * N_RES)
        o_ref[0, 0, 0, pl.ds(p0, WINDOW_SUB * N_RES), :] = nat_s[...].astype(BF16)
        return 0

    lax.fori_loop(0, L // WINDOW_SUB, super_tile, 0)


def _dilated(qkv):
    B, _, R, L, _ = qkv.shape
    blk = (1, 1, R, L, LANES)
    out_blk = (1, 1, 1, R * L, LANES)
    scr = pltpu.VMEM((3, R, WINDOW_SUB, LANES), F32)
    return pl.pallas_call(
        _dil_kernel,
        out_shape=jax.ShapeDtypeStruct((B, N_PAIRS, 1, R * L, LANES), BF16),
        grid=(B, N_PAIRS),
        in_specs=[
            pl.BlockSpec(blk, lambda b, p: (b, p, 0, 0, 0)),
            pl.BlockSpec(blk, lambda b, p: (b, N_PAIRS + p, 0, 0, 0)),
            pl.BlockSpec(blk, lambda b, p: (b, 2 * N_PAIRS + p, 0, 0, 0)),
        ],
        out_specs=pl.BlockSpec(out_blk, lambda b, p: (b, p, 0, 0, 0)),
        scratch_shapes=[
            pltpu.VMEM((1, 2, 256, LANES), BF16), pltpu.VMEM((1, 2, 256, LANES), BF16),
            pltpu.VMEM((2, 2, 512, LANES), BF16), scr, scr,
            pltpu.VMEM((WINDOW_SUB * R, LANES), F32)],
        compiler_params=pltpu.CompilerParams(
            dimension_semantics=("parallel", "parallel"),
            vmem_limit_bytes=VMEM_LIMIT),
        name="dilated",
    )(qkv, qkv, qkv)


def _layer_norm(y, g, b):
    mu = jnp.mean(y, axis=-1, keepdims=True)
    d = y - mu
    var = jnp.mean(d * d, axis=-1, keepdims=True)
    return d * lax.rsqrt(var + LN_EPS) * g + b


def _ffn_kernel(o_ref, x_ref, wo_ref, g1_ref, b1_ref, w1_ref, w2_ref, g2_ref, b2_ref, out_ref, *, n_sub):
    tm = x_ref.shape[1]
    sm = tm // n_sub
    subs = range(n_sub)
    rows = [pl.ds(h * sm, sm) for h in subs]
    mix = [jnp.dot(jnp.concatenate([o_ref[0, j, 0, rows[h], :] for j in range(N_PAIRS)], axis=1),
                   wo_ref[...], preferred_element_type=F32) for h in subs]
    x1 = [_layer_norm(ALPHA * x_ref[0, rows[h], :] + mix[h], g1_ref[...], b1_ref[...]) for h in subs]
    x1b = [t.astype(BF16) for t in x1]
    f = [None] * n_sub
    for c in range(D_FF // D_MODEL):
        cs = slice(c * D_MODEL, (c + 1) * D_MODEL)
        hid = [jnp.dot(x1b[h], w1_ref[:, cs], preferred_element_type=F32) for h in subs]
        hid = [jnp.square(jnp.maximum(t, 0.0)).astype(BF16) for t in hid]
        for h in subs:
            fc = jnp.dot(hid[h], w2_ref[cs, :], preferred_element_type=F32)
            f[h] = fc if f[h] is None else f[h] + fc
    for h in subs:
        out_ref[0, rows[h], :] = _layer_norm(ALPHA * x1[h] + f[h], g2_ref[...], b2_ref[...])


def _ffn_block(o, x, wo, g1, b1, w1, w2, g2, b2, tm=1024, n_sub=4):
    B, S, D = x.shape
    vec = lambda t: t.reshape(1, D)
    row_spec = pl.BlockSpec((1, tm, D), lambda b, i: (b, i, 0))
    return pl.pallas_call(
        functools.partial(_ffn_kernel, n_sub=n_sub),
        out_shape=jax.ShapeDtypeStruct((B, S, D), F32),
        grid=(B, S // tm),
        in_specs=[
            pl.BlockSpec((1, N_PAIRS, 1, tm, LANES), lambda b, i: (b, 0, 0, i, 0)),
            row_spec,
            _const_spec((D, D)), _const_spec((1, D)), _const_spec((1, D)),
            _const_spec((D, D_FF)), _const_spec((D_FF, D)),
            _const_spec((1, D)), _const_spec((1, D)),
        ],
        out_specs=row_spec,
        compiler_params=pltpu.CompilerParams(
            dimension_semantics=("parallel", "parallel"),
            vmem_limit_bytes=VMEM_LIMIT),
        name="ffn_block",
    )(o, x, wo, vec(g1), vec(b1), w1, w2, vec(g2), vec(b2))


def _rotary_tables(S, n_res):
    half = ROPE_DIM // 2
    inv_freq = ROPE_THETA ** (-jnp.arange(half, dtype=F32) / half)
    L = S // n_res
    pos = (n_res * jnp.arange(L, dtype=jnp.int32)[None, :]
           + jnp.arange(n_res, dtype=jnp.int32)[:, None]).reshape(S)
    ang = pos.astype(F32)[:, None] * inv_freq[None, :]
    cs = jnp.concatenate([jnp.cos(ang), jnp.sin(ang)], axis=1)
    d = jnp.arange(LANES, dtype=jnp.int32) % HEAD_DIM
    src = jnp.arange(2 * half, dtype=jnp.int32)[:, None]
    f = (d % half)[None, :]
    d = d[None, :]
    e_c = jnp.logical_and(src == f, d < ROPE_DIM).astype(F32)
    e_sa = -jnp.logical_and(src == half + f, d < half).astype(F32)
    e_sb = jnp.logical_and(src == half + f, jnp.logical_and(d >= half, d < ROPE_DIM)).astype(F32)
    expand = lambda e: jnp.dot(cs, e, precision=lax.Precision.HIGHEST)
    c = expand(e_c) + (d >= ROPE_DIM).astype(F32)
    return tuple(t.reshape(n_res, L, LANES) for t in (c, expand(e_sa), expand(e_sb)))


def kernel(x, w_qkv_0, w_o_0, ln1_g_0, ln1_b_0, w_ff1_0, w_ff2_0, ln2_g_0, ln2_b_0,
           w_qkv_1, w_o_1, ln1_g_1, ln1_b_1, w_ff1_1, w_ff2_1, ln2_g_1, ln2_b_1):
    B, S, D = x.shape
    bf = lambda w: w.astype(BF16)

    qkv0 = _qkv_proj(x, bf(w_qkv_0), LOG2E / math.sqrt(HEAD_DIM))
    o0 = _stick_breaking(qkv0)
    x = _ffn_block(o0, x, bf(w_o_0), ln1_g_0, ln1_b_0, bf(w_ff1_0), bf(w_ff2_0), ln2_g_0, ln2_b_0)

    qkv1 = _qkv_proj_perm(x, bf(w_qkv_1), LOG2E / math.sqrt(HEAD_DIM), _rotary_tables(S, N_RES))
    o1 = _dilated(qkv1)
    x = _ffn_block(o1, x, bf(w_o_1), ln1_g_1, ln1_b_1, bf(w_ff1_1), bf(w_ff2_1), ln2_g_1, ln2_b_1)
    return x
```

```python
import functools
import math

import jax
import jax.numpy as jnp
from jax import lax
from jax.experimental import pallas as pl
from jax.experimental.pallas import tpu as pltpu

D_MODEL = 1024
N_HEADS = 16
HEAD_DIM = 64
D_FF = 4096
ROPE_THETA = 500000.0
ROPE_DIM = 16
DEPTH = 2
ALPHA = (2 * DEPTH) ** 0.25
LN_EPS = 1e-5
WINDOW_SUB = 128
N_RES = 16

LANES = 128
N_PAIRS = D_MODEL // LANES
N_COLS = 3 * N_PAIRS
VMEM_LIMIT = 56 * 1024 * 1024

LOG2E = 1.4426950408889634
SB_DONE = 152.0
SB_TQ = 128
SB_HK = 160
SB_PAD = 256

F32 = jnp.float32
BF16 = jnp.bfloat16
MASKED = -1e30


def _const_spec(shape):
    nd = len(shape)
    return pl.BlockSpec(shape, lambda *_: (0,) * nd, pipeline_mode=pl.Buffered(1))


def _qkv_kernel(x_ref, w_ref, o_ref, *, q_scale):
    x = x_ref[0].astype(BF16)
    for part in range(3):
        y = jnp.dot(x, w_ref[:, part * D_MODEL:(part + 1) * D_MODEL],
                    preferred_element_type=F32)
        for j in range(N_PAIRS):
            col = y[:, j * LANES:(j + 1) * LANES]
            if part == 0:
                col = col * q_scale
            o_ref[0, part * N_PAIRS + j, 0] = col.astype(BF16)


def _qkv_proj(x, w_bf16, q_scale, tm=512):
    B, S, D = x.shape
    return pl.pallas_call(
        functools.partial(_qkv_kernel, q_scale=q_scale),
        out_shape=jax.ShapeDtypeStruct((B, N_COLS, 1, S, LANES), BF16),
        grid=(B, S // tm),
        in_specs=[pl.BlockSpec((1, tm, D), lambda b, i: (b, i, 0)), _const_spec((D, 3 * D))],
        out_specs=pl.BlockSpec((1, N_COLS, 1, tm, LANES), lambda b, i: (b, 0, 0, i, 0)),
        compiler_params=pltpu.CompilerParams(
            dimension_semantics=("parallel", "parallel"),
            vmem_limit_bytes=VMEM_LIMIT),
        name="qkv",
    )(x, w_bf16)


def _qkv_perm_kernel(x_ref, w_ref, c_ref, sa_ref, sb_ref, o_ref, xs_ref, ys_ref, *, q_scale):
    tm = x_ref.shape[1]
    tl = tm // N_RES
    q4 = tm // 4
    for j in range(N_PAIRS):
        xs_ref[j] = x_ref[0, :, j * LANES:(j + 1) * LANES]
    for j in range(N_PAIRS):
        for c in range(4):
            ys_ref[j, pl.ds(c * q4, q4), :] = xs_ref[j, pl.ds(c, q4, stride=4), :]
    slabs = [(c, a) for c in range(4) for a in range(4)]
    xp = jnp.concatenate(
        [jnp.concatenate([ys_ref[j, pl.ds(c * q4 + a, tl, stride=4), :] for j in range(N_PAIRS)], axis=1)
         for c, a in slabs], axis=0).astype(BF16)
    regroup = lambda t: jnp.concatenate([t[4 * a + c] for c, a in slabs], axis=0)
    cos, sa, sb = regroup(c_ref), regroup(sa_ref), regroup(sb_ref)
    for part in range(3):
        y = jnp.dot(xp, w_ref[:, part * D_MODEL:(part + 1) * D_MODEL],
                    preferred_element_type=F32)
        for j in range(N_PAIRS):
            col = y[:, j * LANES:(j + 1) * LANES]
            if part < 2:
                col = (col * cos
                       + pltpu.roll(col, LANES - ROPE_DIM // 2, 1) * sa
                       + pltpu.roll(col, ROPE_DIM // 2, 1) * sb)
            if part == 0:
                col = col * q_scale
            col = col.astype(BF16)
            for n, (c, a) in enumerate(slabs):
                o_ref[0, part * N_PAIRS + j, 4 * a + c] = col[n * tl:(n + 1) * tl]


def _qkv_proj_perm(x, w_bf16, q_scale, tables, tm=512):
    B, S, D = x.shape
    L = S // N_RES
    tl = tm // N_RES
    tab_spec = pl.BlockSpec((N_RES, tl, LANES), lambda b, i: (0, i, 0))
    return pl.pallas_call(
        functools.partial(_qkv_perm_kernel, q_scale=q_scale),
        out_shape=jax.ShapeDtypeStruct((B, N_COLS, N_RES, L, LANES), BF16),
        grid=(B, S // tm),
        in_specs=[pl.BlockSpec((1, tm, D), lambda b, i: (b, i, 0)), _const_spec((D, 3 * D)),
                  tab_spec, tab_spec, tab_spec],
        out_specs=pl.BlockSpec((1, N_COLS, N_RES, tl, LANES), lambda b, i: (b, 0, 0, i, 0)),
        scratch_shapes=[pltpu.VMEM((N_PAIRS, tm, LANES), F32), pltpu.VMEM((N_PAIRS, tm, LANES), F32)],
        compiler_params=pltpu.CompilerParams(
            dimension_semantics=("parallel", "parallel"),
            vmem_limit_bytes=VMEM_LIMIT),
        name="qkv_rot",
    )(x, w_bf16, *tables)


def _softplus2(z):
    neg_abs = lax.bitcast_convert_type(
        lax.bitcast_convert_type(z, jnp.uint32) | jnp.uint32(0x80000000), F32)
    return jnp.maximum(z, 0.0) + jnp.log2(1.0 + jnp.exp2(neg_abs))


def _sb_kernel(q_ref, k_ref, v_ref, o_ref, kp_ref, vp_ref, acc_ref, c_ref, *, group):
    S = q_ref.shape[3]
    tq, hk, pad = SB_TQ, SB_HK, SB_PAD
    nblk = S // tq

    kp_ref[pl.ds(0, pad), :] = jnp.zeros((pad, LANES), BF16)
    vp_ref[pl.ds(0, pad), :] = jnp.zeros((pad, LANES), BF16)

    def copy_in(i, _):
        r0 = pl.multiple_of(i * 512, 512)
        kp_ref[pl.ds(pad + r0, 512), :] = k_ref[0, 0, 0, pl.ds(r0, 512), :]
        vp_ref[pl.ds(pad + r0, 512), :] = v_ref[0, 0, 0, pl.ds(r0, 512), :]
        return 0
    lax.fori_loop(0, S // 512, copy_in, 0)

    lane = lax.broadcasted_iota(jnp.int32, (1, LANES), 1)
    head0 = lane < HEAD_DIM
    sq_r = lax.broadcasted_iota(jnp.int32, (tq, tq), 0)
    sq_c = lax.broadcasted_iota(jnp.int32, (tq, tq), 1)
    causal_bias = jnp.where(sq_r < sq_c, 0.0, MASKED).astype(BF16)
    one_hot = (sq_r == sq_c).astype(BF16)
    one_hot2 = jnp.concatenate([one_hot, one_hot], axis=1)

    def suffix_ones(n):
        return (lax.broadcasted_iota(jnp.int32, (n, n), 1)
                >= lax.broadcasted_iota(jnp.int32, (n, n), 0)).astype(BF16)
    t_diag, t_hist = suffix_ones(tq), suffix_ones(hk)
    dim_head0 = lax.broadcasted_iota(jnp.int32, (LANES, tq), 0) < HEAD_DIM

    def q_stack(q0):
        q = q_ref[0, 0, 0, pl.ds(q0, tq), :]
        zero = jnp.zeros_like(q)
        return jnp.concatenate([jnp.where(head0, q, zero), jnp.where(head0, zero, q)], axis=0).T

    def scores(qs, k0, n, bias=None):
        kc = kp_ref[pl.ds(k0, n), :]
        if bias is not None:
            kc = jnp.concatenate([kc, bias], axis=1)
            qs = jnp.concatenate([qs, one_hot2], axis=0)
        return jnp.dot(kc, qs, preferred_element_type=F32)

    def weighted_values(k0, n, w):
        vc = vp_ref[pl.ds(k0, n), :]
        return lax.dot_general(vc, w.astype(BF16), (((0,), (0,)), ((), ())),
                               preferred_element_type=F32)

    def chunk(qs, k0, n, t_mat, carry):
        z = scores(qs, k0, n)
        r = jnp.dot(t_mat, _softplus2(z).astype(BF16), preferred_element_type=F32)
        return weighted_values(k0, n, jnp.exp2(z - r - carry)), r[0:1, :]

    def fast(blks):
        n = len(blks)
        scored, summed, cmins = {}, {}, [None] * n
        for t in range(n + 2):
            if t < n:
                q0 = pl.multiple_of(blks[t] * tq, tq)
                kd = pl.multiple_of(pad + q0, tq)
                kh = pl.multiple_of(pad + q0 - hk, 32)
                qs = q_stack(q0)
                scored[t] = (scores(qs, kd, tq, causal_bias), scores(qs, kh, hk), kd, kh)
            if 0 <= t - 1 < n:
                zd, zh, kd, kh = scored.pop(t - 1)
                spd = _softplus2(zd)
                sph = _softplus2(zh)
                rd = jnp.dot(t_diag, spd.astype(BF16), preferred_element_type=F32)
                rh = jnp.dot(t_hist, sph.astype(BF16), preferred_element_type=F32)
                summed[t - 1] = (zd, zh, rd, rh, jnp.sum(spd, axis=0, keepdims=True), kd, kh)
            if 0 <= t - 2 < n:
                g = t - 2
                zd, zh, rd, rh, cd, kd, kh = summed.pop(g)
                wd = jnp.exp2(zd - rd)
                wh = jnp.exp2(zh - rh - cd)
                acc_ref[g] = weighted_values(kd, tq, wd) + weighted_values(kh, hk, wh)
                c = cd + rh[0:1, :]
                c_ref[g] = c
                cmins[g] = jnp.min(c)
        return cmins

    def earlier_keys(g, blk, cmin):
        q0 = pl.multiple_of(blk * tq, tq)

        def cond(carry):
            hi, cmin = carry
            return jnp.logical_and(hi > 0, cmin < SB_DONE)

        def body(carry):
            hi, _ = carry
            qs = q_stack(q0)
            k0 = pl.multiple_of(pad + hi - tq, 32)
            c = c_ref[g]
            acc, c_n = chunk(qs, k0, tq, t_diag, c)
            acc_ref[g] += acc
            c = c + c_n
            c_ref[g] = c
            return hi - tq, jnp.min(c)

        lax.while_loop(cond, body, (q0 - hk, cmin))

    def write_out(g, blk):
        q0 = pl.multiple_of(blk * tq, tq)
        acc = acc_ref[g]
        o_t = jnp.where(dim_head0, acc[:, :tq], acc[:, tq:])
        o_ref[0, 0, 0, pl.ds(q0, tq), :] = o_t.T.astype(BF16)

    def block_group(i, _):
        blks = [i * group + g for g in range(group)]
        cmins = fast(blks)
        for g in range(group):
            earlier_keys(g, blks[g], cmins[g])
        for g in range(group):
            write_out(g, blks[g])
        return 0

    lax.fori_loop(0, nblk // group, block_group, 0)


def _stick_breaking(qkv, group=64):
    B, _, _, S, _ = qkv.shape
    assert S % (SB_TQ * group) == 0 and S % 512 == 0
    blk = (1, 1, 1, S, LANES)
    return pl.pallas_call(
        functools.partial(_sb_kernel, group=group),
        out_shape=jax.ShapeDtypeStruct((B, N_PAIRS, 1, S, LANES), BF16),
        grid=(B, N_PAIRS),
        in_specs=[
            pl.BlockSpec(blk, lambda b, p: (b, p, 0, 0, 0)),
            pl.BlockSpec(blk, lambda b, p: (b, N_PAIRS + p, 0, 0, 0)),
            pl.BlockSpec(blk, lambda b, p: (b, 2 * N_PAIRS + p, 0, 0, 0)),
        ],
        out_specs=pl.BlockSpec(blk, lambda b, p: (b, p, 0, 0, 0)),
        scratch_shapes=[
            pltpu.VMEM((SB_PAD + S, LANES), BF16), pltpu.VMEM((SB_PAD + S, LANES), BF16),
            pltpu.VMEM((group, LANES, 2 * SB_TQ), F32), pltpu.VMEM((group, 1, 2 * SB_TQ), F32)],
        compiler_params=pltpu.CompilerParams(
            dimension_semantics=("parallel", "parallel"),
            vmem_limit_bytes=VMEM_LIMIT),
        name="stick_breaking",
    )(qkv, qkv, qkv)


_DIL_TILES = ((1, 1, 128), (4, 4, 32), (8, 16, 16))


def _dil_bias(branch, slab0, first):
    n_q, n_k, q_rows = _DIL_TILES[branch]
    tq, tk = n_q * q_rows, 2 * n_k * q_rows
    kj = lax.broadcasted_iota(jnp.int32, (tk, tq), 0)
    qi = lax.broadcasted_iota(jnp.int32, (tk, tq), 1)
    sh = q_rows.bit_length() - 1
    a, i = qi >> sh, qi & (q_rows - 1)
    a2, j = kj >> (sh + 1), kj & (2 * q_rows - 1)
    dq = 0 if first else q_rows
    dist = n_k * (dq + i - j) + (slab0 + a - a2)
    ok = jnp.logical_and(dist >= 0, dist <= WINDOW_SUB)
    return jnp.where(ok, 0.0, MASKED).astype(BF16)


def _dil_kernel(q_ref, k_ref, v_ref, o_ref, b16_ref, b4_ref, b1_ref, lw_s, o_s, nat_s):
    L = q_ref.shape[3]
    lane = lax.broadcasted_iota(jnp.int32, (1, LANES), 1)
    head0 = lane < HEAD_DIM
    bias_refs = (b16_ref, b4_ref, b1_ref)
    for branch, b_ref in enumerate(bias_refs):
        n_q, n_k, _ = _DIL_TILES[branch]
        for half in range(n_k // n_q):
            b_ref[half, 0] = _dil_bias(branch, half * n_q, False)
            b_ref[half, 1] = _dil_bias(branch, half * n_q, True)
    eye = (lax.broadcasted_iota(jnp.int32, (LANES, LANES), 0)
           == lax.broadcasted_iota(jnp.int32, (LANES, LANES), 1)).astype(BF16)
    one_hot2 = jnp.concatenate([eye, eye], axis=1)
    dim_head0 = lax.broadcasted_iota(jnp.int32, (LANES, LANES), 0) < HEAD_DIM

    def q_aug(q):
        zero = jnp.zeros_like(q)
        qs = jnp.concatenate([jnp.where(head0, q, zero), jnp.where(head0, zero, q)], axis=0)
        return jnp.concatenate([qs.T, one_hot2], axis=0)

    def tiles(branch, specs, lag):
        n_q, n_k, q_rows = _DIL_TILES[branch]
        stride = N_RES // n_k

        def scores(spec):
            slab0, lq0, _ = spec
            lq0 = pl.multiple_of(lq0, q_rows)
            first = (lq0 < q_rows).astype(jnp.int32)
            kq0 = pl.multiple_of(jnp.maximum(lq0 - q_rows, 0), q_rows)
            half = (slab0 // stride) // n_q if n_k > n_q else 0
            k_slabs = [(slab0 % stride) + stride * a for a in range(n_k)] if n_k == n_q else list(range(n_k))
            q_slabs = [slab0 + stride * a for a in range(n_q)]
            q = jnp.concatenate([q_ref[0, 0, sl, pl.ds(lq0, q_rows), :] for sl in q_slabs], axis=0)
            k = jnp.concatenate([k_ref[0, 0, sl, pl.ds(kq0, 2 * q_rows), :] for sl in k_slabs], axis=0)
            k = jnp.concatenate([k, bias_refs[branch][half, first]], axis=1)
            s = jnp.dot(k, q_aug(q), preferred_element_type=F32)
            return s, kq0, k_slabs, q_slabs

        def finish(spec, s, kq0, k_slabs, q_slabs):
            _, _, out_off = spec
            v = jnp.concatenate([v_ref[0, 0, sl, pl.ds(kq0, 2 * q_rows), :] for sl in k_slabs], axis=0)
            m = jnp.max(s, axis=0, keepdims=True)
            p = jnp.exp2(s - m)
            l = jnp.sum(p, axis=0, keepdims=True)
            num = lax.dot_general(v, p.astype(BF16), (((0,), (0,)), ((), ())),
                                  preferred_element_type=F32)
            num = num * (1.0 / l)
            lw = jnp.broadcast_to(m + jnp.log2(l), (LANES, 2 * LANES))
            o_t = jnp.where(dim_head0, num[:, :LANES], num[:, LANES:]).T
            lw_t = jnp.where(dim_head0, lw[:, :LANES], lw[:, LANES:]).T
            out_off = pl.multiple_of(out_off, q_rows)
            for n, sl in enumerate(q_slabs):
                rows = slice(n * q_rows, (n + 1) * q_rows)
                lw_s[branch, sl, pl.ds(out_off, q_rows), :] = lw_t[rows]
                o_s[branch, sl, pl.ds(out_off, q_rows), :] = o_t[rows]

        pending = []
        for t in range(len(specs) + lag):
            if t < len(specs):
                pending.append(scores(specs[t]))
            if t >= lag:
                finish(specs[t - lag], *pending[t - lag])

    def super_tile(jt, _):
        l0 = jt * WINDOW_SUB
        tiles(0, [(r, l0, 0) for r in range(N_RES)], 3)
        tiles(1, [(c, l0 + 32 * u, 32 * u) for c in range(4) for u in range(4)], 3)
        tiles(2, [(8 * half, l0 + 16 * u, 16 * u) for u in range(8) for half in range(2)], 3)
        for r in range(N_RES):
            w0, w1, w2 = lw_s[0, r], lw_s[1, r], lw_s[2, r]
            wm = jnp.maximum(jnp.maximum(w0, w1), w2)
            e0, e1, e2 = jnp.exp2(w0 - wm), jnp.exp2(w1 - wm), jnp.exp2(w2 - wm)
            num = e0 * o_s[0, r] + e1 * o_s[1, r] + e2 * o_s[2, r]
            nat_s[pl.ds(r, WINDOW_SUB, stride=N_RES), :] = num / (e0 + e1 + e2)
        p0 = pl.multiple_of(l0 * N_RES, WINDOW_SUB * N_RES)
        o_ref[0, 0, 0, pl.ds(p0, WINDOW_SUB * N_RES), :] = nat_s[...].astype(BF16)
        return 0

    lax.fori_loop(0, L // WINDOW_SUB, super_tile, 0)


def _dilated(qkv):
    B, _, R, L, _ = qkv.shape
    blk = (1, 1, R, L, LANES)
    out_blk = (1, 1, 1, R * L, LANES)
    scr = pltpu.VMEM((3, R, WINDOW_SUB, LANES), F32)
    return pl.pallas_call(
        _dil_kernel,
        out_shape=jax.ShapeDtypeStruct((B, N_PAIRS, 1, R * L, LANES), BF16),
        grid=(B, N_PAIRS),
        in_specs=[
            pl.BlockSpec(blk, lambda b, p: (b, p, 0, 0, 0)),
            pl.BlockSpec(blk, lambda b, p: (b, N_PAIRS + p, 0, 0, 0)),
            pl.BlockSpec(blk, lambda b, p: (b, 2 * N_PAIRS + p, 0, 0, 0)),
        ],
        out_specs=pl.BlockSpec(out_blk, lambda b, p: (b, p, 0, 0, 0)),
        scratch_shapes=[
            pltpu.VMEM((1, 2, 256, LANES), BF16), pltpu.VMEM((1, 2, 256, LANES), BF16),
            pltpu.VMEM((2, 2, 512, LANES), BF16), scr, scr,
            pltpu.VMEM((WINDOW_SUB * R, LANES), F32)],
        compiler_params=pltpu.CompilerParams(
            dimension_semantics=("parallel", "parallel"),
            vmem_limit_bytes=VMEM_LIMIT),
        name="dilated",
    )(qkv, qkv, qkv)


def _layer_norm(y, g, b):
    mu = jnp.mean(y, axis=-1, keepdims=True)
    d = y - mu
    var = jnp.mean(d * d, axis=-1, keepdims=True)
    return d * lax.rsqrt(var + LN_EPS) * g + b


def _ffn_kernel(o_ref, x_ref, wo_ref, g1_ref, b1_ref, w1_ref, w2_ref, g2_ref, b2_ref, out_ref, *, n_sub):
    tm = x_ref.shape[1]
    sm = tm // n_sub
    subs = range(n_sub)
    rows = [pl.ds(h * sm, sm) for h in subs]
    mix = [jnp.dot(jnp.concatenate([o_ref[0, j, 0, rows[h], :] for j in range(N_PAIRS)], axis=1),
                   wo_ref[...], preferred_element_type=F32) for h in subs]
    x1 = [_layer_norm(ALPHA * x_ref[0, rows[h], :] + mix[h], g1_ref[...], b1_ref[...]) for h in subs]
    x1b = [t.astype(BF16) for t in x1]
    f = [None] * n_sub
    for c in range(D_FF // D_MODEL):
        cs = slice(c * D_MODEL, (c + 1) * D_MODEL)
        hid = [jnp.dot(x1b[h], w1_ref[:, cs], preferred_element_type=F32) for h in subs]
        hid = [jnp.square(jnp.maximum(t, 0.0)).astype(BF16) for t in hid]
        for h in subs:
            fc = jnp.dot(hid[h], w2_ref[cs, :], preferred_element_type=F32)
            f[h] = fc if f[h] is None else f[h] + fc
    for h in subs:
        out_ref[0, rows[h], :] = _layer_norm(ALPHA * x1[h] + f[h], g2_ref[...], b2_ref[...])


def _ffn_block(o, x, wo, g1, b1, w1, w2, g2, b2, tm=1024, n_sub=4):
    B, S, D = x.shape
    vec = lambda t: t.reshape(1, D)
    row_spec = pl.BlockSpec((1, tm, D), lambda b, i: (b, i, 0))
    return pl.pallas_call(
        functools.partial(_ffn_kernel, n_sub=n_sub),
        out_shape=jax.ShapeDtypeStruct((B, S, D), F32),
        grid=(B, S // tm),
        in_specs=[
            pl.BlockSpec((1, N_PAIRS, 1, tm, LANES), lambda b, i: (b, 0, 0, i, 0)),
            row_spec,
            _const_spec((D, D)), _const_spec((1, D)), _const_spec((1, D)),
            _const_spec((D, D_FF)), _const_spec((D_FF, D)),
            _const_spec((1, D)), _const_spec((1, D)),
        ],
        out_specs=row_spec,
        compiler_params=pltpu.CompilerParams(
            dimension_semantics=("parallel", "parallel"),
            vmem_limit_bytes=VMEM_LIMIT),
        name="ffn_block",
    )(o, x, wo, vec(g1), vec(b1), w1, w2, vec(g2), vec(b2))


def _rotary_tables(S, n_res):
    half = ROPE_DIM // 2
    inv_freq = ROPE_THETA ** (-jnp.arange(half, dtype=F32) / half)
    L = S // n_res
    pos = (n_res * jnp.arange(L, dtype=jnp.int32)[None, :]
           + jnp.arange(n_res, dtype=jnp.int32)[:, None]).reshape(S)
    ang = pos.astype(F32)[:, None] * inv_freq[None, :]
    cs = jnp.concatenate([jnp.cos(ang), jnp.sin(ang)], axis=1)
    d = jnp.arange(LANES, dtype=jnp.int32) % HEAD_DIM
    src = jnp.arange(2 * half, dtype=jnp.int32)[:, None]
    f = (d % half)[None, :]
    d = d[None, :]
    e_c = jnp.logical_and(src == f, d < ROPE_DIM).astype(F32)
    e_sa = -jnp.logical_and(src == half + f, d < half).astype(F32)
    e_sb = jnp.logical_and(src == half + f, jnp.logical_and(d >= half, d < ROPE_DIM)).astype(F32)
    expand = lambda e: jnp.dot(cs, e, precision=lax.Precision.HIGHEST)
    c = expand(e_c) + (d >= ROPE_DIM).astype(F32)
    return tuple(t.reshape(n_res, L, LANES) for t in (c, expand(e_sa), expand(e_sb)))


def kernel(x, w_qkv_0, w_o_0, ln1_g_0, ln1_b_0, w_ff1_0, w_ff2_0, ln2_g_0, ln2_b_0,
           w_qkv_1, w_o_1, ln1_g_1, ln1_b_1, w_ff1_1, w_ff2_1, ln2_g_1, ln2_b_1):
    B, S, D = x.shape
    bf = lambda w: w.astype(BF16)

    qkv0 = _qkv_proj(x, bf(w_qkv_0), LOG2E / math.sqrt(HEAD_DIM))
    o0 = _stick_breaking(qkv0)
    x = _ffn_block(o0, x, bf(w_o_0), ln1_g_0, ln1_b_0, bf(w_ff1_0), bf(w_ff2_0), ln2_g_0, ln2_b_0)

    qkv1 = _qkv_proj_perm(x, bf(w_qkv_1), LOG2E / math.sqrt(HEAD_DIM), _rotary_tables(S, N_RES))
    o1 = _dilated(qkv1)
    x = _ffn_block(o1, x, bf(w_o_1), ln1_g_1, ln1_b_1, bf(w_ff1_1), bf(w_ff2_1), ln2_g_1, ln2_b_1)
    return x
```

```python
import functools
import math

import jax
import jax.numpy as jnp
from jax import lax
from jax.experimental import pallas as pl
from jax.experimental.pallas import tpu as pltpu

D_MODEL = 1024
N_HEADS = 16
HEAD_DIM = 64
D_FF = 4096
ROPE_THETA = 500000.0
ROPE_DIM = 16
DEPTH = 2
ALPHA = (2 * DEPTH) ** 0.25
LN_EPS = 1e-5
WINDOW_SUB = 128
N_RES = 16

LANES = 128
N_PAIRS = D_MODEL // LANES
N_COLS = 3 * N_PAIRS
VMEM_LIMIT = 56 * 1024 * 1024

LOG2E = 1.4426950408889634
SB_DONE = 152.0
SB_TQ = 128
SB_HK = 160
SB_PAD = 256

F32 = jnp.float32
BF16 = jnp.bfloat16
MASKED = -1e30


def _const_spec(shape):
    nd = len(shape)
    return pl.BlockSpec(shape, lambda *_: (0,) * nd, pipeline_mode=pl.Buffered(1))


def _cast_weight_once(w_ref, wb_ref):
    @pl.when(jnp.logical_and(pl.program_id(0) == 0, pl.program_id(1) == 0))
    def _():
        for part in range(3):
            cols = slice(part * D_MODEL, (part + 1) * D_MODEL)
            wb_ref[:, cols] = w_ref[:, cols].astype(BF16)


def _qkv_kernel(x_ref, w32_ref, o_ref, w_ref, *, q_scale):
    _cast_weight_once(w32_ref, w_ref)
    x = x_ref[0].astype(BF16)
    for part in range(3):
        y = jnp.dot(x, w_ref[:, part * D_MODEL:(part + 1) * D_MODEL],
                    preferred_element_type=F32)
        for j in range(N_PAIRS):
            col = y[:, j * LANES:(j + 1) * LANES]
            if part == 0:
                col = col * q_scale
            o_ref[0, part * N_PAIRS + j, 0] = col.astype(BF16)


def _qkv_proj(x, w, q_scale, tm=512):
    B, S, D = x.shape
    return pl.pallas_call(
        functools.partial(_qkv_kernel, q_scale=q_scale),
        out_shape=jax.ShapeDtypeStruct((B, N_COLS, 1, S, LANES), BF16),
        grid=(B, S // tm),
        in_specs=[pl.BlockSpec((1, tm, D), lambda b, i: (b, i, 0)), _const_spec((D, 3 * D))],
        out_specs=pl.BlockSpec((1, N_COLS, 1, tm, LANES), lambda b, i: (b, 0, 0, i, 0)),
        scratch_shapes=[pltpu.VMEM((D, 3 * D), BF16)],
        compiler_params=pltpu.CompilerParams(
            dimension_semantics=("arbitrary", "arbitrary"),
            vmem_limit_bytes=VMEM_LIMIT),
        name="qkv",
    )(x, w)


def _qkv_perm_kernel(x_ref, w32_ref, c_ref, sa_ref, sb_ref, o_ref, xs_ref, ys_ref, w_ref, *, q_scale):
    _cast_weight_once(w32_ref, w_ref)
    tm = x_ref.shape[1]
    tl = tm // N_RES
    q4 = tm // 4
    for j in range(N_PAIRS):
        xs_ref[j] = x_ref[0, :, j * LANES:(j + 1) * LANES]
    for j in range(N_PAIRS):
        for c in range(4):
            ys_ref[j, pl.ds(c * q4, q4), :] = xs_ref[j, pl.ds(c, q4, stride=4), :]
    slabs = [(c, a) for c in range(4) for a in range(4)]
    xp = jnp.concatenate(
        [jnp.concatenate([ys_ref[j, pl.ds(c * q4 + a, tl, stride=4), :] for j in range(N_PAIRS)], axis=1)
         for c, a in slabs], axis=0).astype(BF16)
    regroup = lambda t: jnp.concatenate([t[4 * a + c] for c, a in slabs], axis=0)
    cos, sa, sb = regroup(c_ref), regroup(sa_ref), regroup(sb_ref)
    for part in range(3):
        y = jnp.dot(xp, w_ref[:, part * D_MODEL:(part + 1) * D_MODEL],
                    preferred_element_type=F32)
        for j in range(N_PAIRS):
            col = y[:, j * LANES:(j + 1) * LANES]
            if part < 2:
                col = (col * cos
                       + pltpu.roll(col, LANES - ROPE_DIM // 2, 1) * sa
                       + pltpu.roll(col, ROPE_DIM // 2, 1) * sb)
            if part == 0:
                col = col * q_scale
            col = col.astype(BF16)
            for n, (c, a) in enumerate(slabs):
                o_ref[0, part * N_PAIRS + j, 4 * a + c] = col[n * tl:(n + 1) * tl]


def _qkv_proj_perm(x, w, q_scale, tables, tm=512):
    B, S, D = x.shape
    L = S // N_RES
    tl = tm // N_RES
    tab_spec = pl.BlockSpec((N_RES, tl, LANES), lambda b, i: (0, i, 0))
    return pl.pallas_call(
        functools.partial(_qkv_perm_kernel, q_scale=q_scale),
        out_shape=jax.ShapeDtypeStruct((B, N_COLS, N_RES, L, LANES), BF16),
        grid=(B, S // tm),
        in_specs=[pl.BlockSpec((1, tm, D), lambda b, i: (b, i, 0)), _const_spec((D, 3 * D)),
                  tab_spec, tab_spec, tab_spec],
        out_specs=pl.BlockSpec((1, N_COLS, N_RES, tl, LANES), lambda b, i: (b, 0, 0, i, 0)),
        scratch_shapes=[pltpu.VMEM((N_PAIRS, tm, LANES), F32), pltpu.VMEM((N_PAIRS, tm, LANES), F32),
                        pltpu.VMEM((D, 3 * D), BF16)],
        compiler_params=pltpu.CompilerParams(
            dimension_semantics=("arbitrary", "arbitrary"),
            vmem_limit_bytes=VMEM_LIMIT),
        name="qkv_rot",
    )(x, w, *tables)


def _softplus2(z):
    neg_abs = lax.bitcast_convert_type(
        lax.bitcast_convert_type(z, jnp.uint32) | jnp.uint32(0x80000000), F32)
    return jnp.maximum(z, 0.0) + jnp.log2(1.0 + jnp.exp2(neg_abs))


def _sb_kernel(q_ref, k_ref, v_ref, o_ref, kp_ref, vp_ref, acc_ref, c_ref, *, group):
    S = q_ref.shape[3]
    tq, hk, pad = SB_TQ, SB_HK, SB_PAD
    nblk = S // tq

    kp_ref[pl.ds(0, pad), :] = jnp.zeros((pad, LANES), BF16)
    vp_ref[pl.ds(0, pad), :] = jnp.zeros((pad, LANES), BF16)

    def copy_in(i, _):
        r0 = pl.multiple_of(i * 512, 512)
        kp_ref[pl.ds(pad + r0, 512), :] = k_ref[0, 0, 0, pl.ds(r0, 512), :]
        vp_ref[pl.ds(pad + r0, 512), :] = v_ref[0, 0, 0, pl.ds(r0, 512), :]
        return 0
    lax.fori_loop(0, S // 512, copy_in, 0)

    lane = lax.broadcasted_iota(jnp.int32, (1, LANES), 1)
    head0 = lane < HEAD_DIM
    sq_r = lax.broadcasted_iota(jnp.int32, (tq, tq), 0)
    sq_c = lax.broadcasted_iota(jnp.int32, (tq, tq), 1)
    causal_bias = jnp.where(sq_r < sq_c, 0.0, MASKED).astype(BF16)
    one_hot = (sq_r == sq_c).astype(BF16)
    one_hot2 = jnp.concatenate([one_hot, one_hot], axis=1)

    def suffix_ones(n):
        return (lax.broadcasted_iota(jnp.int32, (n, n), 1)
                >= lax.broadcasted_iota(jnp.int32, (n, n), 0)).astype(BF16)
    t_diag, t_hist = suffix_ones(tq), suffix_ones(hk)
    dim_head0 = lax.broadcasted_iota(jnp.int32, (LANES, tq), 0) < HEAD_DIM

    def q_stack(q0):
        q = q_ref[0, 0, 0, pl.ds(q0, tq), :]
        zero = jnp.zeros_like(q)
        return jnp.concatenate([jnp.where(head0, q, zero), jnp.where(head0, zero, q)], axis=0).T

    def scores(qs, k0, n, bias=None):
        kc = kp_ref[pl.ds(k0, n), :]
        if bias is not None:
            kc = jnp.concatenate([kc, bias], axis=1)
            qs = jnp.concatenate([qs, one_hot2], axis=0)
        return jnp.dot(kc, qs, preferred_element_type=F32)

    def weighted_values(k0, n, w):
        vc = vp_ref[pl.ds(k0, n), :]
        return lax.dot_general(vc, w.astype(BF16), (((0,), (0,)), ((), ())),
                               preferred_element_type=F32)

    def chunk(qs, k0, n, t_mat, carry):
        z = scores(qs, k0, n)
        r = jnp.dot(t_mat, _softplus2(z).astype(BF16), preferred_element_type=F32)
        return weighted_values(k0, n, jnp.exp2(z - r - carry)), r[0:1, :]

    def fast(blks):
        n = len(blks)
        scored, summed, cmins = {}, {}, [None] * n
        for t in range(n + 2):
            if t < n:
                q0 = pl.multiple_of(blks[t] * tq, tq)
                kd = pl.multiple_of(pad + q0, tq)
                kh = pl.multiple_of(pad + q0 - hk, 32)
                qs = q_stack(q0)
                scored[t] = (scores(qs, kd, tq, causal_bias), scores(qs, kh, hk), kd, kh)
            if 0 <= t - 1 < n:
                zd, zh, kd, kh = scored.pop(t - 1)
                spd = _softplus2(zd)
                sph = _softplus2(zh)
                rd = jnp.dot(t_diag, spd.astype(BF16), preferred_element_type=F32)
                rh = jnp.dot(t_hist, sph.astype(BF16), preferred_element_type=F32)
                summed[t - 1] = (zd, zh, rd, rh, jnp.sum(spd, axis=0, keepdims=True), kd, kh)
            if 0 <= t - 2 < n:
                g = t - 2
                zd, zh, rd, rh, cd, kd, kh = summed.pop(g)
                wd = jnp.exp2(zd - rd)
                wh = jnp.exp2(zh - rh - cd)
                acc_ref[g] = weighted_values(kd, tq, wd) + weighted_values(kh, hk, wh)
                c = cd + rh[0:1, :]
                c_ref[g] = c
                cmins[g] = jnp.min(c)
        return cmins

    def earlier_keys(g, blk, cmin):
        q0 = pl.multiple_of(blk * tq, tq)

        def cond(carry):
            hi, cmin = carry
            return jnp.logical_and(hi > 0, cmin < SB_DONE)

        def body(carry):
            hi, _ = carry
            qs = q_stack(q0)
            k0 = pl.multiple_of(pad + hi - tq, 32)
            c = c_ref[g]
            acc, c_n = chunk(qs, k0, tq, t_diag, c)
            acc_ref[g] += acc
            c = c + c_n
            c_ref[g] = c
            return hi - tq, jnp.min(c)

        lax.while_loop(cond, body, (q0 - hk, cmin))

    def write_out(g, blk):
        q0 = pl.multiple_of(blk * tq, tq)
        acc = acc_ref[g]
        o_t = jnp.where(dim_head0, acc[:, :tq], acc[:, tq:])
        o_ref[0, 0, 0, pl.ds(q0, tq), :] = o_t.T.astype(BF16)

    def block_group(i, _):
        blks = [i * group + g for g in range(group)]
        cmins = fast(blks)
        for g in range(group):
            earlier_keys(g, blks[g], cmins[g])
        for g in range(group):
            write_out(g, blks[g])
        return 0

    lax.fori_loop(0, nblk // group, block_group, 0)


def _stick_breaking(qkv, group=64):
    B, _, _, S, _ = qkv.shape
    assert S % (SB_TQ * group) == 0 and S % 512 == 0
    blk = (1, 1, 1, S, LANES)
    return pl.pallas_call(
        functools.partial(_sb_kernel, group=group),
        out_shape=jax.ShapeDtypeStruct((B, N_PAIRS, 1, S, LANES), BF16),
        grid=(B, N_PAIRS),
        in_specs=[
            pl.BlockSpec(blk, lambda b, p: (b, p, 0, 0, 0)),
            pl.BlockSpec(blk, lambda b, p: (b, N_PAIRS + p, 0, 0, 0)),
            pl.BlockSpec(blk, lambda b, p: (b, 2 * N_PAIRS + p, 0, 0, 0)),
        ],
        out_specs=pl.BlockSpec(blk, lambda b, p: (b, p, 0, 0, 0)),
        scratch_shapes=[
            pltpu.VMEM((SB_PAD + S, LANES), BF16), pltpu.VMEM((SB_PAD + S, LANES), BF16),
            pltpu.VMEM((group, LANES, 2 * SB_TQ), F32), pltpu.VMEM((group, 1, 2 * SB_TQ), F32)],
        compiler_params=pltpu.CompilerParams(
            dimension_semantics=("parallel", "parallel"),
            vmem_limit_bytes=VMEM_LIMIT),
        name="stick_breaking",
    )(qkv, qkv, qkv)


_DIL_TILES = ((1, 1, 128), (4, 4, 32), (8, 16, 16))


def _dil_bias(branch, slab0, first):
    n_q, n_k, q_rows = _DIL_TILES[branch]
    tq, tk = n_q * q_rows, 2 * n_k * q_rows
    kj = lax.broadcasted_iota(jnp.int32, (tk, tq), 0)
    qi = lax.broadcasted_iota(jnp.int32, (tk, tq), 1)
    sh = q_rows.bit_length() - 1
    a, i = qi >> sh, qi & (q_rows - 1)
    a2, j = kj >> (sh + 1), kj & (2 * q_rows - 1)
    dq = 0 if first else q_rows
    dist = n_k * (dq + i - j) + (slab0 + a - a2)
    ok = jnp.logical_and(dist >= 0, dist <= WINDOW_SUB)
    return jnp.where(ok, 0.0, MASKED).astype(BF16)


def _dil_kernel(q_ref, k_ref, v_ref, o_ref, b16_ref, b4_ref, b1_ref, lw_s, o_s, nat_s):
    L = q_ref.shape[3]
    lane = lax.broadcasted_iota(jnp.int32, (1, LANES), 1)
    head0 = lane < HEAD_DIM
    bias_refs = (b16_ref, b4_ref, b1_ref)
    for branch, b_ref in enumerate(bias_refs):
        n_q, n_k, _ = _DIL_TILES[branch]
        for half in range(n_k // n_q):
            b_ref[half, 0] = _dil_bias(branch, half * n_q, False)
            b_ref[half, 1] = _dil_bias(branch, half * n_q, True)
    eye = (lax.broadcasted_iota(jnp.int32, (LANES, LANES), 0)
           == lax.broadcasted_iota(jnp.int32, (LANES, LANES), 1)).astype(BF16)
    one_hot2 = jnp.concatenate([eye, eye], axis=1)
    dim_head0 = lax.broadcasted_iota(jnp.int32, (LANES, LANES), 0) < HEAD_DIM

    def q_aug(q):
        zero = jnp.zeros_like(q)
        qs = jnp.concatenate([jnp.where(head0, q, zero), jnp.where(head0, zero, q)], axis=0)
        return jnp.concatenate([qs.T, one_hot2], axis=0)

    def tiles(branch, specs, lag):
        n_q, n_k, q_rows = _DIL_TILES[branch]
        stride = N_RES // n_k

        def scores(spec):
            slab0, lq0, _ = spec
            lq0 = pl.multiple_of(lq0, q_rows)
            first = (lq0 < q_rows).astype(jnp.int32)
            kq0 = pl.multiple_of(jnp.maximum(lq0 - q_rows, 0), q_rows)
            half = (slab0 // stride) // n_q if n_k > n_q else 0
            k_slabs = [(slab0 % stride) + stride * a for a in range(n_k)] if n_k == n_q else list(range(n_k))
            q_slabs = [slab0 + stride * a for a in range(n_q)]
            q = jnp.concatenate([q_ref[0, 0, sl, pl.ds(lq0, q_rows), :] for sl in q_slabs], axis=0)
            k = jnp.concatenate([k_ref[0, 0, sl, pl.ds(kq0, 2 * q_rows), :] for sl in k_slabs], axis=0)
            k = jnp.concatenate([k, bias_refs[branch][half, first]], axis=1)
            s = jnp.dot(k, q_aug(q), preferred_element_type=F32)
            return s, kq0, k_slabs, q_slabs

        def finish(spec, s, kq0, k_slabs, q_slabs):
            _, _, out_off = spec
            v = jnp.concatenate([v_ref[0, 0, sl, pl.ds(kq0, 2 * q_rows), :] for sl in k_slabs], axis=0)
            m = jnp.max(s, axis=0, keepdims=True)
            p = jnp.exp2(s - m)
            l = jnp.sum(p, axis=0, keepdims=True)
            num = lax.dot_general(v, p.astype(BF16), (((0,), (0,)), ((), ())),
                                  preferred_element_type=F32)
            num = num * (1.0 / l)
            lw = jnp.broadcast_to(m + jnp.log2(l), (LANES, 2 * LANES))
            o_t = jnp.where(dim_head0, num[:, :LANES], num[:, LANES:]).T
            lw_t = jnp.where(dim_head0, lw[:, :LANES], lw[:, LANES:]).T
            out_off = pl.multiple_of(out_off, q_rows)
            for n, sl in enumerate(q_slabs):
                rows = slice(n * q_rows, (n + 1) * q_rows)
                lw_s[branch, sl, pl.ds(out_off, q_rows), :] = lw_t[rows]
                o_s[branch, sl, pl.ds(out_off, q_rows), :] = o_t[rows]

        pending = []
        for t in range(len(specs) + lag):
            if t < len(specs):
                pending.append(scores(specs[t]))
            if t >= lag:
                finish(specs[t - lag], *pending[t - lag])

    def super_tile(jt, _):
        l0 = jt * WINDOW_SUB
        tiles(0, [(r, l0, 0) for r in range(N_RES)], 3)
        tiles(1, [(c, l0 + 32 * u, 32 * u) for c in range(4) for u in range(4)], 3)
        tiles(2, [(8 * half, l0 + 16 * u, 16 * u) for u in range(8) for half in range(2)], 3)
        for r in range(N_RES):
            w0, w1, w2 = lw_s[0, r], lw_s[1, r], lw_s[2, r]
            wm = jnp.maximum(jnp.maximum(w0, w1), w2)
            e0, e1, e2 = jnp.exp2(w0 - wm), jnp.exp2(w1 - wm), jnp.exp2(w2 - wm)
            num = e0 * o_s[0, r] + e1 * o_s[1, r] + e2 * o_s[2, r]
            nat_s[pl.ds(r, WINDOW_SUB, stride=N_RES), :] = num / (e0 + e1 + e2)
        p0 = pl.multiple_of(l0 * N_RES, WINDOW_SUB * N_RES)
        o_ref[0, 0, 0, pl.ds(p0, WINDOW_SUB * N_RES), :] = nat_s[...].astype(BF16)
        return 0

    lax.fori_loop(0, L // WINDOW_SUB, super_tile, 0)


def _dilated(qkv):
    B, _, R, L, _ = qkv.shape
    blk = (1, 1, R, L, LANES)
    out_blk = (1, 1, 1, R * L, LANES)
    scr = pltpu.VMEM((3, R, WINDOW_SUB, LANES), F32)
    return pl.pallas_call(
        _dil_kernel,
        out_shape=jax.ShapeDtypeStruct((B, N_PAIRS, 1, R * L, LANES), BF16),
        grid=(B, N_PAIRS),
        in_specs=[
            pl.BlockSpec(blk, lambda b, p: (b, p, 0, 0, 0)),
            pl.BlockSpec(blk, lambda b, p: (b, N_PAIRS + p, 0, 0, 0)),
            pl.BlockSpec(blk, lambda b, p: (b, 2 * N_PAIRS + p, 0, 0, 0)),
        ],
        out_specs=pl.BlockSpec(out_blk, lambda b, p: (b, p, 0, 0, 0)),
        scratch_shapes=[
            pltpu.VMEM((1, 2, 256, LANES), BF16), pltpu.VMEM((1, 2, 256, LANES), BF16),
            pltpu.VMEM((2, 2, 512, LANES), BF16), scr, scr,
            pltpu.VMEM((WINDOW_SUB * R, LANES), F32)],
        compiler_params=pltpu.CompilerParams(
            dimension_semantics=("parallel", "parallel"),
            vmem_limit_bytes=VMEM_LIMIT),
        name="dilated",
    )(qkv, qkv, qkv)


def _layer_norm(y, g, b):
    mu = jnp.mean(y, axis=-1, keepdims=True)
    d = y - mu
    var = jnp.mean(d * d, axis=-1, keepdims=True)
    return d * lax.rsqrt(var + LN_EPS) * g + b


def _ffn_kernel(o_ref, x_ref, wo_ref, g1_ref, b1_ref, w1_ref, w2_ref, g2_ref, b2_ref, out_ref, *, n_sub):
    tm = x_ref.shape[1]
    sm = tm // n_sub
    subs = range(n_sub)
    rows = [pl.ds(h * sm, sm) for h in subs]
    mix = [jnp.dot(jnp.concatenate([o_ref[0, j, 0, rows[h], :] for j in range(N_PAIRS)], axis=1),
                   wo_ref[...], preferred_element_type=F32) for h in subs]
    x1 = [_layer_norm(ALPHA * x_ref[0, rows[h], :] + mix[h], g1_ref[...], b1_ref[...]) for h in subs]
    x1b = [t.astype(BF16) for t in x1]
    f = [None] * n_sub
    for c in range(D_FF // D_MODEL):
        cs = slice(c * D_MODEL, (c + 1) * D_MODEL)
        hid = [jnp.dot(x1b[h], w1_ref[:, cs], preferred_element_type=F32) for h in subs]
        hid = [jnp.square(jnp.maximum(t, 0.0)).astype(BF16) for t in hid]
        for h in subs:
            fc = jnp.dot(hid[h], w2_ref[cs, :], preferred_element_type=F32)
            f[h] = fc if f[h] is None else f[h] + fc
    for h in subs:
        out_ref[0, rows[h], :] = _layer_norm(ALPHA * x1[h] + f[h], g2_ref[...], b2_ref[...])


def _ffn_block(o, x, wo, g1, b1, w1, w2, g2, b2, tm=1024, n_sub=4):
    B, S, D = x.shape
    vec = lambda t: t.reshape(1, D)
    row_spec = pl.BlockSpec((1, tm, D), lambda b, i: (b, i, 0))
    return pl.pallas_call(
        functools.partial(_ffn_kernel, n_sub=n_sub),
        out_shape=jax.ShapeDtypeStruct((B, S, D), F32),
        grid=(B, S // tm),
        in_specs=[
            pl.BlockSpec((1, N_PAIRS, 1, tm, LANES), lambda b, i: (b, 0, 0, i, 0)),
            row_spec,
            _const_spec((D, D)), _const_spec((1, D)), _const_spec((1, D)),
            _const_spec((D, D_FF)), _const_spec((D_FF, D)),
            _const_spec((1, D)), _const_spec((1, D)),
        ],
        out_specs=row_spec,
        compiler_params=pltpu.CompilerParams(
            dimension_semantics=("parallel", "parallel"),
            vmem_limit_bytes=VMEM_LIMIT),
        name="ffn_block",
    )(o, x, wo, vec(g1), vec(b1), w1, w2, vec(g2), vec(b2))


def _rotary_tables(S, n_res):
    half = ROPE_DIM // 2
    inv_freq = ROPE_THETA ** (-jnp.arange(half, dtype=F32) / half)
    L = S // n_res
    pos = (n_res * jnp.arange(L, dtype=jnp.int32)[None, :]
           + jnp.arange(n_res, dtype=jnp.int32)[:, None]).reshape(S)
    ang = pos.astype(F32)[:, None] * inv_freq[None, :]
    cs = jnp.concatenate([jnp.cos(ang), jnp.sin(ang)], axis=1)
    d = jnp.arange(LANES, dtype=jnp.int32) % HEAD_DIM
    src = jnp.arange(2 * half, dtype=jnp.int32)[:, None]
    f = (d % half)[None, :]
    d = d[None, :]
    e_c = jnp.logical_and(src == f, d < ROPE_DIM).astype(F32)
    e_sa = -jnp.logical_and(src == half + f, d < half).astype(F32)
    e_sb = jnp.logical_and(src == half + f, jnp.logical_and(d >= half, d < ROPE_DIM)).astype(F32)
    expand = lambda e: jnp.dot(cs, e, precision=lax.Precision.HIGHEST)
    c = expand(e_c) + (d >= ROPE_DIM).astype(F32)
    return tuple(t.reshape(n_res, L, LANES) for t in (c, expand(e_sa), expand(e_sb)))


def kernel(x, w_qkv_0, w_o_0, ln1_g_0, ln1_b_0, w_ff1_0, w_ff2_0, ln2_g_0, ln2_b_0,
           w_qkv_1, w_o_1, ln1_g_1, ln1_b_1, w_ff1_1, w_ff2_1, ln2_g_1, ln2_b_1):
    B, S, D = x.shape
    bf = lambda w: w.astype(BF16)

    qkv0 = _qkv_proj(x, w_qkv_0, LOG2E / math.sqrt(HEAD_DIM))
    o0 = _stick_breaking(qkv0)
    x = _ffn_block(o0, x, bf(w_o_0), ln1_g_0, ln1_b_0, bf(w_ff1_0), bf(w_ff2_0), ln2_g_0, ln2_b_0)

    qkv1 = _qkv_proj_perm(x, w_qkv_1, LOG2E / math.sqrt(HEAD_DIM), _rotary_tables(S, N_RES))
    o1 = _dilated(qkv1)
    x = _ffn_block(o1, x, bf(w_o_1), ln1_g_1, ln1_b_1, bf(w_ff1_1), bf(w_ff2_1), ln2_g_1, ln2_b_1)
    return x
```

```python
import functools
import math

import jax
import jax.numpy as jnp
from jax import lax
from jax.experimental import pallas as pl
from jax.experimental.pallas import tpu as pltpu

D_MODEL = 1024
N_HEADS = 16
HEAD_DIM = 64
D_FF = 4096
ROPE_THETA = 500000.0
ROPE_DIM = 16
DEPTH = 2
ALPHA = (2 * DEPTH) ** 0.25
LN_EPS = 1e-5
WINDOW_SUB = 128
N_RES = 16

LANES = 128
N_PAIRS = D_MODEL // LANES
N_COLS = 3 * N_PAIRS
VMEM_LIMIT = 56 * 1024 * 1024

LOG2E = 1.4426950408889634
SB_DONE = 152.0
SB_TQ = 128
SB_HK = 160
SB_PAD = 256

F32 = jnp.float32
BF16 = jnp.bfloat16
MASKED = -1e30


def _const_spec(shape):
    nd = len(shape)
    return pl.BlockSpec(shape, lambda *_: (0,) * nd, pipeline_mode=pl.Buffered(1))


def _cast_weight_once(w_ref, wb_ref):
    @pl.when(jnp.logical_and(pl.program_id(0) == 0, pl.program_id(1) == 0))
    def _():
        for part in range(3):
            cols = slice(part * D_MODEL, (part + 1) * D_MODEL)
            wb_ref[:, cols] = w_ref[:, cols].astype(BF16)


def _qkv_kernel(x_ref, w32_ref, o_ref, w_ref, *, q_scale):
    _cast_weight_once(w32_ref, w_ref)
    x = x_ref[0].astype(BF16)
    for part in range(3):
        y = jnp.dot(x, w_ref[:, part * D_MODEL:(part + 1) * D_MODEL],
                    preferred_element_type=F32)
        for j in range(N_PAIRS):
            col = y[:, j * LANES:(j + 1) * LANES]
            if part == 0:
                col = col * q_scale
            o_ref[0, part * N_PAIRS + j, 0] = col.astype(BF16)


def _qkv_proj(x, w, q_scale, tm=512):
    B, S, D = x.shape
    return pl.pallas_call(
        functools.partial(_qkv_kernel, q_scale=q_scale),
        out_shape=jax.ShapeDtypeStruct((B, N_COLS, 1, S, LANES), BF16),
        grid=(B, S // tm),
        in_specs=[pl.BlockSpec((1, tm, D), lambda b, i: (b, i, 0)), _const_spec((D, 3 * D))],
        out_specs=pl.BlockSpec((1, N_COLS, 1, tm, LANES), lambda b, i: (b, 0, 0, i, 0)),
        scratch_shapes=[pltpu.VMEM((D, 3 * D), BF16)],
        compiler_params=pltpu.CompilerParams(
            dimension_semantics=("arbitrary", "arbitrary"),
            vmem_limit_bytes=VMEM_LIMIT),
        name="qkv",
    )(x, w)


def _qkv_perm_kernel(x_ref, w32_ref, c_ref, sa_ref, sb_ref, o_ref, xs_ref, ys_ref, w_ref, *, q_scale):
    _cast_weight_once(w32_ref, w_ref)
    tm = x_ref.shape[1]
    tl = tm // N_RES
    q4 = tm // 4
    for j in range(N_PAIRS):
        xs_ref[j] = x_ref[0, :, j * LANES:(j + 1) * LANES]
    for j in range(N_PAIRS):
        for c in range(4):
            ys_ref[j, pl.ds(c * q4, q4), :] = xs_ref[j, pl.ds(c, q4, stride=4), :]
    slabs = [(c, a) for c in range(4) for a in range(4)]
    xp = jnp.concatenate(
        [jnp.concatenate([ys_ref[j, pl.ds(c * q4 + a, tl, stride=4), :] for j in range(N_PAIRS)], axis=1)
         for c, a in slabs], axis=0).astype(BF16)
    regroup = lambda t: jnp.concatenate([t[4 * a + c] for c, a in slabs], axis=0)
    cos, sa, sb = regroup(c_ref), regroup(sa_ref), regroup(sb_ref)
    for part in range(3):
        y = jnp.dot(xp, w_ref[:, part * D_MODEL:(part + 1) * D_MODEL],
                    preferred_element_type=F32)
        for j in range(N_PAIRS):
            col = y[:, j * LANES:(j + 1) * LANES]
            if part < 2:
                col = (col * cos
                       + pltpu.roll(col, LANES - ROPE_DIM // 2, 1) * sa
                       + pltpu.roll(col, ROPE_DIM // 2, 1) * sb)
            if part == 0:
                col = col * q_scale
            col = col.astype(BF16)
            for n, (c, a) in enumerate(slabs):
                o_ref[0, part * N_PAIRS + j, 4 * a + c] = col[n * tl:(n + 1) * tl]


def _qkv_proj_perm(x, w, q_scale, tables, tm=512):
    B, S, D = x.shape
    L = S // N_RES
    tl = tm // N_RES
    tab_spec = pl.BlockSpec((N_RES, tl, LANES), lambda b, i: (0, i, 0))
    return pl.pallas_call(
        functools.partial(_qkv_perm_kernel, q_scale=q_scale),
        out_shape=jax.ShapeDtypeStruct((B, N_COLS, N_RES, L, LANES), BF16),
        grid=(B, S // tm),
        in_specs=[pl.BlockSpec((1, tm, D), lambda b, i: (b, i, 0)), _const_spec((D, 3 * D)),
                  tab_spec, tab_spec, tab_spec],
        out_specs=pl.BlockSpec((1, N_COLS, N_RES, tl, LANES), lambda b, i: (b, 0, 0, i, 0)),
        scratch_shapes=[pltpu.VMEM((N_PAIRS, tm, LANES), F32), pltpu.VMEM((N_PAIRS, tm, LANES), F32),
                        pltpu.VMEM((D, 3 * D), BF16)],
        compiler_params=pltpu.CompilerParams(
            dimension_semantics=("arbitrary", "arbitrary"),
            vmem_limit_bytes=VMEM_LIMIT),
        name="qkv_rot",
    )(x, w, *tables)


def _softplus2(z):
    return jnp.where(z > 64.0, z, jnp.log2(1.0 + jnp.exp2(z)))


def _sb_kernel(q_ref, k_ref, v_ref, o_ref, kp_ref, vp_ref, acc_ref, c_ref, *, group):
    S = q_ref.shape[3]
    tq, hk, pad = SB_TQ, SB_HK, SB_PAD
    nblk = S // tq

    kp_ref[pl.ds(0, pad), :] = jnp.zeros((pad, LANES), BF16)
    vp_ref[pl.ds(0, pad), :] = jnp.zeros((pad, LANES), BF16)

    def copy_in(i, _):
        r0 = pl.multiple_of(i * 512, 512)
        kp_ref[pl.ds(pad + r0, 512), :] = k_ref[0, 0, 0, pl.ds(r0, 512), :]
        vp_ref[pl.ds(pad + r0, 512), :] = v_ref[0, 0, 0, pl.ds(r0, 512), :]
        return 0
    lax.fori_loop(0, S // 512, copy_in, 0)

    lane = lax.broadcasted_iota(jnp.int32, (1, LANES), 1)
    head0 = lane < HEAD_DIM
    sq_r = lax.broadcasted_iota(jnp.int32, (tq, tq), 0)
    sq_c = lax.broadcasted_iota(jnp.int32, (tq, tq), 1)
    causal_bias = jnp.where(sq_r < sq_c, 0.0, MASKED).astype(BF16)
    one_hot = (sq_r == sq_c).astype(BF16)
    one_hot2 = jnp.concatenate([one_hot, one_hot], axis=1)

    def suffix_ones(n):
        return (lax.broadcasted_iota(jnp.int32, (n, n), 1)
                >= lax.broadcasted_iota(jnp.int32, (n, n), 0)).astype(BF16)
    t_diag, t_hist = suffix_ones(tq), suffix_ones(hk)
    dim_head0 = lax.broadcasted_iota(jnp.int32, (LANES, tq), 0) < HEAD_DIM

    def q_stack(q0):
        q = q_ref[0, 0, 0, pl.ds(q0, tq), :]
        zero = jnp.zeros_like(q)
        return jnp.concatenate([jnp.where(head0, q, zero), jnp.where(head0, zero, q)], axis=0).T

    def scores(qs, k0, n, bias=None):
        kc = kp_ref[pl.ds(k0, n), :]
        if bias is not None:
            kc = jnp.concatenate([kc, bias], axis=1)
            qs = jnp.concatenate([qs, one_hot2], axis=0)
        return jnp.dot(kc, qs, preferred_element_type=F32)

    def weighted_values(k0, n, w):
        vc = vp_ref[pl.ds(k0, n), :]
        return lax.dot_general(vc, w.astype(BF16), (((0,), (0,)), ((), ())),
                               preferred_element_type=F32)

    def chunk(qs, k0, n, t_mat, carry):
        z = scores(qs, k0, n)
        r = jnp.dot(t_mat, _softplus2(z).astype(BF16), preferred_element_type=F32)
        return weighted_values(k0, n, jnp.exp2(z - r - carry)), r[0:1, :]

    def fast(blks):
        n = len(blks)
        scored, summed, cmins = {}, {}, [None] * n
        for t in range(n + 2):
            if t < n:
                q0 = pl.multiple_of(blks[t] * tq, tq)
                kd = pl.multiple_of(pad + q0, tq)
                kh = pl.multiple_of(pad + q0 - hk, 32)
                qs = q_stack(q0)
                scored[t] = (scores(qs, kd, tq, causal_bias), scores(qs, kh, hk), kd, kh)
            if 0 <= t - 1 < n:
                zd, zh, kd, kh = scored.pop(t - 1)
                spd = _softplus2(zd)
                sph = _softplus2(zh)
                rd = jnp.dot(t_diag, spd.astype(BF16), preferred_element_type=F32)
                rh = jnp.dot(t_hist, sph.astype(BF16), preferred_element_type=F32)
                summed[t - 1] = (zd, zh, rd, rh, rd[0:1, :], kd, kh)
            if 0 <= t - 2 < n:
                g = t - 2
                zd, zh, rd, rh, cd, kd, kh = summed.pop(g)
                wd = jnp.exp2(zd - rd)
                wh = jnp.exp2(zh - rh - cd)
                acc_ref[g] = weighted_values(kd, tq, wd) + weighted_values(kh, hk, wh)
                c = cd + rh[0:1, :]
                c_ref[g] = c
                cmins[g] = jnp.min(c)
        return cmins

    def earlier_keys(g, blk, cmin):
        q0 = pl.multiple_of(blk * tq, tq)

        def cond(carry):
            hi, cmin = carry
            return jnp.logical_and(hi > 0, cmin < SB_DONE)

        def body(carry):
            hi, _ = carry
            qs = q_stack(q0)
            k0 = pl.multiple_of(pad + hi - tq, 32)
            c = c_ref[g]
            acc, c_n = chunk(qs, k0, tq, t_diag, c)
            acc_ref[g] += acc
            c = c + c_n
            c_ref[g] = c
            return hi - tq, jnp.min(c)

        lax.while_loop(cond, body, (q0 - hk, cmin))

    def write_out(g, blk):
        q0 = pl.multiple_of(blk * tq, tq)
        acc = acc_ref[g]
        o_t = jnp.where(dim_head0, acc[:, :tq], acc[:, tq:])
        o_ref[0, 0, 0, pl.ds(q0, tq), :] = o_t.T.astype(BF16)

    def block_group(i, _):
        blks = [i * group + g for g in range(group)]
        cmins = fast(blks)
        for g in range(group):
            earlier_keys(g, blks[g], cmins[g])
        for g in range(group):
            write_out(g, blks[g])
        return 0

    lax.fori_loop(0, nblk // group, block_group, 0)


def _stick_breaking(qkv, group=64):
    B, _, _, S, _ = qkv.shape
    assert S % (SB_TQ * group) == 0 and S % 512 == 0
    blk = (1, 1, 1, S, LANES)
    return pl.pallas_call(
        functools.partial(_sb_kernel, group=group),
        out_shape=jax.ShapeDtypeStruct((B, N_PAIRS, 1, S, LANES), BF16),
        grid=(B, N_PAIRS),
        in_specs=[
            pl.BlockSpec(blk, lambda b, p: (b, p, 0, 0, 0)),
            pl.BlockSpec(blk, lambda b, p: (b, N_PAIRS + p, 0, 0, 0)),
            pl.BlockSpec(blk, lambda b, p: (b, 2 * N_PAIRS + p, 0, 0, 0)),
        ],
        out_specs=pl.BlockSpec(blk, lambda b, p: (b, p, 0, 0, 0)),
        scratch_shapes=[
            pltpu.VMEM((SB_PAD + S, LANES), BF16), pltpu.VMEM((SB_PAD + S, LANES), BF16),
            pltpu.VMEM((group, LANES, 2 * SB_TQ), F32), pltpu.VMEM((group, 1, 2 * SB_TQ), F32)],
        compiler_params=pltpu.CompilerParams(
            dimension_semantics=("parallel", "parallel"),
            vmem_limit_bytes=VMEM_LIMIT),
        name="stick_breaking",
    )(qkv, qkv, qkv)


_DIL_TILES = ((1, 1, 128), (4, 4, 32), (8, 16, 16))


def _dil_bias(branch, slab0, first):
    n_q, n_k, q_rows = _DIL_TILES[branch]
    tq, tk = n_q * q_rows, 2 * n_k * q_rows
    kj = lax.broadcasted_iota(jnp.int32, (tk, tq), 0)
    qi = lax.broadcasted_iota(jnp.int32, (tk, tq), 1)
    sh = q_rows.bit_length() - 1
    a, i = qi >> sh, qi & (q_rows - 1)
    a2, j = kj >> (sh + 1), kj & (2 * q_rows - 1)
    dq = 0 if first else q_rows
    dist = n_k * (dq + i - j) + (slab0 + a - a2)
    ok = jnp.logical_and(dist >= 0, dist <= WINDOW_SUB)
    return jnp.where(ok, 0.0, MASKED).astype(BF16)


def _dil_kernel(q_ref, k_ref, v_ref, o_ref, b16_ref, b4_ref, b1_ref, lw_s, o_s, nat_s):
    L = q_ref.shape[3]
    lane = lax.broadcasted_iota(jnp.int32, (1, LANES), 1)
    head0 = lane < HEAD_DIM
    bias_refs = (b16_ref, b4_ref, b1_ref)
    for branch, b_ref in enumerate(bias_refs):
        n_q, n_k, _ = _DIL_TILES[branch]
        for half in range(n_k // n_q):
            b_ref[half, 0] = _dil_bias(branch, half * n_q, False)
            b_ref[half, 1] = _dil_bias(branch, half * n_q, True)
    eye = (lax.broadcasted_iota(jnp.int32, (LANES, LANES), 0)
           == lax.broadcasted_iota(jnp.int32, (LANES, LANES), 1)).astype(BF16)
    one_hot2 = jnp.concatenate([eye, eye], axis=1)
    dim_head0 = lax.broadcasted_iota(jnp.int32, (LANES, LANES), 0) < HEAD_DIM

    def q_aug(q):
        zero = jnp.zeros_like(q)
        qs = jnp.concatenate([jnp.where(head0, q, zero), jnp.where(head0, zero, q)], axis=0)
        return jnp.concatenate([qs.T, one_hot2], axis=0)

    def tiles(branch, specs, lag):
        n_q, n_k, q_rows = _DIL_TILES[branch]
        stride = N_RES // n_k

        def scores(spec):
            slab0, lq0, _ = spec
            lq0 = pl.multiple_of(lq0, q_rows)
            first = (lq0 < q_rows).astype(jnp.int32)
            kq0 = pl.multiple_of(jnp.maximum(lq0 - q_rows, 0), q_rows)
            half = (slab0 // stride) // n_q if n_k > n_q else 0
            k_slabs = [(slab0 % stride) + stride * a for a in range(n_k)] if n_k == n_q else list(range(n_k))
            q_slabs = [slab0 + stride * a for a in range(n_q)]
            q = jnp.concatenate([q_ref[0, 0, sl, pl.ds(lq0, q_rows), :] for sl in q_slabs], axis=0)
            k = jnp.concatenate([k_ref[0, 0, sl, pl.ds(kq0, 2 * q_rows), :] for sl in k_slabs], axis=0)
            k = jnp.concatenate([k, bias_refs[branch][half, first]], axis=1)
            s = jnp.dot(k, q_aug(q), preferred_element_type=F32)
            return s, kq0, k_slabs, q_slabs

        def finish(spec, s, kq0, k_slabs, q_slabs):
            _, _, out_off = spec
            v = jnp.concatenate([v_ref[0, 0, sl, pl.ds(kq0, 2 * q_rows), :] for sl in k_slabs], axis=0)
            m = jnp.max(s, axis=0, keepdims=True)
            p = jnp.exp2(s - m)
            l = jnp.sum(p, axis=0, keepdims=True)
            num = lax.dot_general(v, p.astype(BF16), (((0,), (0,)), ((), ())),
                                  preferred_element_type=F32)
            num = num * (1.0 / l)
            lw = jnp.broadcast_to(m + jnp.log2(l), (LANES, 2 * LANES))
            o_t = jnp.where(dim_head0, num[:, :LANES], num[:, LANES:]).T
            lw_t = jnp.where(dim_head0, lw[:, :LANES], lw[:, LANES:]).T
            out_off = pl.multiple_of(out_off, q_rows)
            for n, sl in enumerate(q_slabs):
                rows = slice(n * q_rows, (n + 1) * q_rows)
                lw_s[branch, sl, pl.ds(out_off, q_rows), :] = lw_t[rows]
                o_s[branch, sl, pl.ds(out_off, q_rows), :] = o_t[rows]

        pending = []
        for t in range(len(specs) + lag):
            if t < len(specs):
                pending.append(scores(specs[t]))
            if t >= lag:
                finish(specs[t - lag], *pending[t - lag])

    def super_tile(jt, _):
        l0 = jt * WINDOW_SUB
        tiles(0, [(r, l0, 0) for r in range(N_RES)], 3)
        tiles(1, [(c, l0 + 32 * u, 32 * u) for c in range(4) for u in range(4)], 3)
        tiles(2, [(8 * half, l0 + 16 * u, 16 * u) for u in range(8) for half in range(2)], 3)
        for r in range(N_RES):
            w0, w1, w2 = lw_s[0, r], lw_s[1, r], lw_s[2, r]
            wm = jnp.maximum(jnp.maximum(w0, w1), w2)
            e0, e1, e2 = jnp.exp2(w0 - wm), jnp.exp2(w1 - wm), jnp.exp2(w2 - wm)
            num = e0 * o_s[0, r] + e1 * o_s[1, r] + e2 * o_s[2, r]
            nat_s[pl.ds(r, WINDOW_SUB, stride=N_RES), :] = num / (e0 + e1 + e2)
        p0 = pl.multiple_of(l0 * N_RES, WINDOW_SUB * N_RES)
        o_ref[0, 0, 0, pl.ds(p0, WINDOW_SUB * N_RES), :] = nat_s[...].astype(BF16)
        return 0

    lax.fori_loop(0, L // WINDOW_SUB, super_tile, 0)


def _dilated(qkv):
    B, _, R, L, _ = qkv.shape
    blk = (1, 1, R, L, LANES)
    out_blk = (1, 1, 1, R * L, LANES)
    scr = pltpu.VMEM((3, R, WINDOW_SUB, LANES), F32)
    return pl.pallas_call(
        _dil_kernel,
        out_shape=jax.ShapeDtypeStruct((B, N_PAIRS, 1, R * L, LANES), BF16),
        grid=(B, N_PAIRS),
        in_specs=[
            pl.BlockSpec(blk, lambda b, p: (b, p, 0, 0, 0)),
            pl.BlockSpec(blk, lambda b, p: (b, N_PAIRS + p, 0, 0, 0)),
            pl.BlockSpec(blk, lambda b, p: (b, 2 * N_PAIRS + p, 0, 0, 0)),
        ],
        out_specs=pl.BlockSpec(out_blk, lambda b, p: (b, p, 0, 0, 0)),
        scratch_shapes=[
            pltpu.VMEM((1, 2, 256, LANES), BF16), pltpu.VMEM((1, 2, 256, LANES), BF16),
            pltpu.VMEM((2, 2, 512, LANES), BF16), scr, scr,
            pltpu.VMEM((WINDOW_SUB * R, LANES), F32)],
        compiler_params=pltpu.CompilerParams(
            dimension_semantics=("parallel", "parallel"),
            vmem_limit_bytes=VMEM_LIMIT),
        name="dilated",
    )(qkv, qkv, qkv)


def _layer_norm(y, g, b):
    mu = jnp.mean(y, axis=-1, keepdims=True)
    d = y - mu
    var = jnp.mean(d * d, axis=-1, keepdims=True)
    return d * lax.rsqrt(var + LN_EPS) * g + b


def _ffn_kernel(o_ref, x_ref, wo_ref, g1_ref, b1_ref, w1_ref, w2_ref, g2_ref, b2_ref, out_ref, *, n_sub):
    tm = x_ref.shape[1]
    sm = tm // n_sub
    subs = range(n_sub)
    rows = [pl.ds(h * sm, sm) for h in subs]
    mix = [jnp.dot(jnp.concatenate([o_ref[0, j, 0, rows[h], :] for j in range(N_PAIRS)], axis=1),
                   wo_ref[...], preferred_element_type=F32) for h in subs]
    x1 = [_layer_norm(ALPHA * x_ref[0, rows[h], :] + mix[h], g1_ref[...], b1_ref[...]) for h in subs]
    x1b = [t.astype(BF16) for t in x1]
    f = [None] * n_sub
    for c in range(D_FF // D_MODEL):
        cs = slice(c * D_MODEL, (c + 1) * D_MODEL)
        hid = [jnp.dot(x1b[h], w1_ref[:, cs], preferred_element_type=F32) for h in subs]
        hid = [jnp.square(jnp.maximum(t, 0.0)).astype(BF16) for t in hid]
        for h in subs:
            fc = jnp.dot(hid[h], w2_ref[cs, :], preferred_element_type=F32)
            f[h] = fc if f[h] is None else f[h] + fc
    for h in subs:
        out_ref[0, rows[h], :] = _layer_norm(ALPHA * x1[h] + f[h], g2_ref[...], b2_ref[...])


def _ffn_block(o, x, wo, g1, b1, w1, w2, g2, b2, tm=1024, n_sub=4):
    B, S, D = x.shape
    vec = lambda t: t.reshape(1, D)
    row_spec = pl.BlockSpec((1, tm, D), lambda b, i: (b, i, 0))
    return pl.pallas_call(
        functools.partial(_ffn_kernel, n_sub=n_sub),
        out_shape=jax.ShapeDtypeStruct((B, S, D), F32),
        grid=(B, S // tm),
        in_specs=[
            pl.BlockSpec((1, N_PAIRS, 1, tm, LANES), lambda b, i: (b, 0, 0, i, 0)),
            row_spec,
            _const_spec((D, D)), _const_spec((1, D)), _const_spec((1, D)),
            _const_spec((D, D_FF)), _const_spec((D_FF, D)),
            _const_spec((1, D)), _const_spec((1, D)),
        ],
        out_specs=row_spec,
        compiler_params=pltpu.CompilerParams(
            dimension_semantics=("parallel", "parallel"),
            vmem_limit_bytes=VMEM_LIMIT),
        name="ffn_block",
    )(o, x, wo, vec(g1), vec(b1), w1, w2, vec(g2), vec(b2))


def _rotary_tables(S, n_res):
    half = ROPE_DIM // 2
    inv_freq = ROPE_THETA ** (-jnp.arange(half, dtype=F32) / half)
    L = S // n_res
    pos = (n_res * jnp.arange(L, dtype=jnp.int32)[None, :]
           + jnp.arange(n_res, dtype=jnp.int32)[:, None]).reshape(S)
    ang = pos.astype(F32)[:, None] * inv_freq[None, :]
    cs = jnp.concatenate([jnp.cos(ang), jnp.sin(ang)], axis=1)
    d = jnp.arange(LANES, dtype=jnp.int32) % HEAD_DIM
    src = jnp.arange(2 * half, dtype=jnp.int32)[:, None]
    f = (d % half)[None, :]
    d = d[None, :]
    e_c = jnp.logical_and(src == f, d < ROPE_DIM).astype(F32)
    e_sa = -jnp.logical_and(src == half + f, d < half).astype(F32)
    e_sb = jnp.logical_and(src == half + f, jnp.logical_and(d >= half, d < ROPE_DIM)).astype(F32)
    expand = lambda e: jnp.dot(cs, e, precision=lax.Precision.HIGHEST)
    c = expand(e_c) + (d >= ROPE_DIM).astype(F32)
    return tuple(t.reshape(n_res, L, LANES) for t in (c, expand(e_sa), expand(e_sb)))


def kernel(x, w_qkv_0, w_o_0, ln1_g_0, ln1_b_0, w_ff1_0, w_ff2_0, ln2_g_0, ln2_b_0,
           w_qkv_1, w_o_1, ln1_g_1, ln1_b_1, w_ff1_1, w_ff2_1, ln2_g_1, ln2_b_1):
    B, S, D = x.shape
    bf = lambda w: w.astype(BF16)

    qkv0 = _qkv_proj(x, w_qkv_0, LOG2E / math.sqrt(HEAD_DIM))
    o0 = _stick_breaking(qkv0)
    x = _ffn_block(o0, x, bf(w_o_0), ln1_g_0, ln1_b_0, bf(w_ff1_0), bf(w_ff2_0), ln2_g_0, ln2_b_0)

    qkv1 = _qkv_proj_perm(x, w_qkv_1, LOG2E / math.sqrt(HEAD_DIM), _rotary_tables(S, N_RES))
    o1 = _dilated(qkv1)
    x = _ffn_block(o1, x, bf(w_o_1), ln1_g_1, ln1_b_1, bf(w_ff1_1), bf(w_ff2_1), ln2_g_1, ln2_b_1)
    return x
```
